```python
import jax
import jax.numpy as jnp
from jax import lax
import numpy as np

D_MODEL = 4096
BATCH = 2
SEQ = 4096
DEPTH = 4

HEAD_DIM = 128
N_HEADS = D_MODEL // HEAD_DIM
FOX_HEADS = N_HEADS // 2
MOBA_HEADS = N_HEADS - FOX_HEADS
FOX_WIDTH = FOX_HEADS * HEAD_DIM
MOBA_WIDTH = MOBA_HEADS * HEAD_DIM
MIX_WIDTH = FOX_WIDTH + MOBA_WIDTH
IN_COLS = 3 * MIX_WIDTH + FOX_HEADS
Q_BLOCK = 128
MOBA_BLOCK = 256
MOBA_TOPK = 3
MOBA_CHUNK = 32
ROPE_THETA = 10000.0
N_EXPERTS = 16
N_GROUPS = 4
EXPERTS_PER_GROUP = N_EXPERTS // N_GROUPS
TOP_K = 2
D_FF = 3 * D_MODEL // 16
EXPERT_ROWS = 256
DN_ALPHA = (2.0 * DEPTH) ** 0.25
DN_BETA = (8.0 * DEPTH) ** -0.25
LN_EPS = 1e-5
NEG_INF = -1e30
ATTN_SCALE = HEAD_DIM ** -0.5

kernel_name = "fox_moba_grouped_moe_deepnorm_trunk"


def layer_norm(x, g, b):
    xf = x.astype(jnp.float32)
    mu = xf.mean(-1, keepdims=True)
    var = jnp.square(xf - mu).mean(-1, keepdims=True)
    return ((xf - mu) * lax.rsqrt(var + LN_EPS) * g.astype(jnp.float32) + b.astype(jnp.float32)).astype(x.dtype)


def rope(x, pos):
    half = HEAD_DIM // 2
    inv = ROPE_THETA ** (-jnp.arange(half, dtype=jnp.float32) / half)
    ang = pos.astype(jnp.float32)[:, None] * inv[None, :]
    cos, sin = jnp.cos(ang), jnp.sin(ang)
    xf = x.astype(jnp.float32)
    x1, x2 = xf[..., :half], xf[..., half:]
    return jnp.concatenate([x1 * cos - x2 * sin, x2 * cos + x1 * sin], axis=-1).astype(x.dtype)


def fox_attention(q, k, v, log_f):
    S = q.shape[2]
    c = jnp.cumsum(log_f, axis=-1)
    outs = []
    for i in range(S // Q_BLOCK):
        lo, hi = i * Q_BLOCK, (i + 1) * Q_BLOCK
        s = jnp.einsum('bhqd,bhkd->bhqk', q[:, :, lo:hi], k[:, :, :hi],
                       preferred_element_type=jnp.float32) * ATTN_SCALE
        s = s + c[:, :, lo:hi, None] - c[:, :, None, :hi]
        causal = (lo + jnp.arange(Q_BLOCK))[:, None] >= jnp.arange(hi)[None, :]
        p = jax.nn.softmax(jnp.where(causal, s, NEG_INF), axis=-1)
        outs.append(jnp.einsum('bhqk,bhkd->bhqd', p.astype(v.dtype), v[:, :, :hi]))
    return jnp.concatenate(outs, axis=2)


def moba_attention(q, k, v):
    B, H, S, Dh = q.shape
    nb = -(-S // MOBA_BLOCK)
    pad = nb * MOBA_BLOCK - S
    kblk = jnp.pad(k, ((0, 0), (0, 0), (0, pad), (0, 0))).reshape(B, H, nb, MOBA_BLOCK, Dh)
    vblk = jnp.pad(v, ((0, 0), (0, 0), (0, pad), (0, 0))).reshape(B, H, nb, MOBA_BLOCK, Dh)
    kbar = kblk.astype(jnp.float32).mean(axis=3)
    n_sel = min(MOBA_TOPK, nb)
    bi = jnp.arange(B)[:, None, None, None]
    hi = jnp.arange(H)[None, :, None, None]
    key_off = jnp.arange(MOBA_BLOCK)

    def chunk(ci):
        t0 = ci * MOBA_CHUNK
        own = t0 // MOBA_BLOCK
        qc = lax.dynamic_slice_in_dim(q, t0, MOBA_CHUNK, axis=2)
        tq = t0 + jnp.arange(MOBA_CHUNK)
        gate = jnp.einsum('bhcd,bhnd->bhcn', qc.astype(jnp.float32), kbar)
        gate = jnp.where(jnp.arange(nb) < own, gate, NEG_INF)
        _, sel = lax.top_k(gate, n_sel)
        sel_ok = jnp.arange(n_sel) < own
        ks = kblk[bi, hi, sel]
        vs = vblk[bi, hi, sel]
        s_sel = jnp.einsum('bhcd,bhcnld->bhcnl', qc, ks,
                           preferred_element_type=jnp.float32) * ATTN_SCALE
        s_sel = jnp.where(sel_ok[:, None], s_sel, NEG_INF).reshape(B, H, MOBA_CHUNK, n_sel * MOBA_BLOCK)
        k_own = lax.dynamic_index_in_dim(kblk, own, axis=2, keepdims=False)
        v_own = lax.dynamic_index_in_dim(vblk, own, axis=2, keepdims=False)
        s_own = jnp.einsum('bhcd,bhld->bhcl', qc, k_own,
                           preferred_element_type=jnp.float32) * ATTN_SCALE
        causal = (own * MOBA_BLOCK + key_off)[None, :] <= tq[:, None]
        s_own = jnp.where(causal, s_own, NEG_INF)
        p = jax.nn.softmax(jnp.concatenate([s_sel, s_own], axis=-1), axis=-1).astype(v.dtype)
        p_sel = p[..., :n_sel * MOBA_BLOCK].reshape(B, H, MOBA_CHUNK, n_sel, MOBA_BLOCK)
        p_own = p[..., n_sel * MOBA_BLOCK:]
        return (jnp.einsum('bhcnl,bhcnld->bhcd', p_sel, vs)
                + jnp.einsum('bhcl,bhld->bhcd', p_own, v_own))

    out = lax.map(chunk, jnp.arange(S // MOBA_CHUNK))
    return out.transpose(1, 2, 0, 3, 4).reshape(B, H, S, Dh)


def route(x_flat, w_router, router_bias):
    N = x_flat.shape[0]
    probs = jax.nn.softmax(x_flat.astype(jnp.float32) @ w_router.astype(jnp.float32), axis=-1)
    sel_scores = probs + router_bias.astype(jnp.float32)
    grouped = sel_scores.reshape(N, N_GROUPS, EXPERTS_PER_GROUP)
    group_score = lax.top_k(grouped, TOP_K)[0].sum(-1)
    _, g_idx = lax.top_k(group_score, 1)
    in_group = (jnp.arange(N_EXPERTS) // EXPERTS_PER_GROUP)[None, :] == g_idx
    _, expert_idx = lax.top_k(jnp.where(in_group, sel_scores, NEG_INF), TOP_K)
    w = jnp.take_along_axis(probs, expert_idx, axis=1)
    return expert_idx, w / w.sum(-1, keepdims=True)


def moe_ffn(x_flat, expert_idx, gate_w, w_gate, w_up, w_down):
    N, D = x_flat.shape
    M = N * TOP_K
    R = EXPERT_ROWS
    n_blocks = -(-M // R) + N_EXPERTS
    P = n_blocks * R
    e_flat = expert_idx.reshape(-1)
    tok_flat = jnp.repeat(jnp.arange(N, dtype=jnp.int32), TOP_K)
    w_flat = gate_w.reshape(-1)
    order = jnp.argsort(e_flat)
    e_s, tok_s, w_s = e_flat[order], tok_flat[order], w_flat[order]
    counts = jnp.bincount(e_flat, length=N_EXPERTS)
    padded = (counts + R - 1) // R * R
    start = jnp.cumsum(counts) - counts
    pend = jnp.cumsum(padded)
    pstart = pend - padded
    dest = pstart[e_s] + jnp.arange(M) - start[e_s]
    row_tok = jnp.full((P,), N, dtype=jnp.int32).at[dest].set(tok_s)
    row_w = jnp.zeros((P,), dtype=w_s.dtype).at[dest].set(w_s)
    blk_expert = jnp.minimum(jnp.searchsorted(pend, jnp.arange(n_blocks) * R, side='right'),
                             N_EXPERTS - 1)
    x_pad = jnp.concatenate([x_flat, jnp.zeros((1, D), x_flat.dtype)], axis=0)
    xg = x_pad[row_tok].reshape(n_blocks, R, D)

    def expert_block(args):
        xb, e = args
        h = jax.nn.silu(xb @ w_gate[e]) * (xb @ w_up[e])
        return h @ w_down[e]

    yb = lax.map(expert_block, (xg, blk_expert)).reshape(P, D)
    out = jax.ops.segment_sum(yb.astype(jnp.float32) * row_w[:, None], row_tok, num_segments=N + 1)
    return out[:N].astype(x_flat.dtype)


def setup_inputs(seed: int = 0) -> dict:
    key = jax.random.key(seed)
    ks = jax.random.split(key, 14)
    F, Mw = FOX_WIDTH, MOBA_WIDTH
    col_scale = jnp.concatenate([
        jnp.ones((2 * F,), jnp.float32), jnp.full((F,), DN_BETA, jnp.float32),
        jnp.ones((2 * Mw,), jnp.float32), jnp.full((Mw,), DN_BETA, jnp.float32),
        jnp.ones((FOX_HEADS,), jnp.float32)])
    w_in = jax.random.normal(ks[0], (DEPTH, D_MODEL, IN_COLS), jnp.float32) * (D_MODEL ** -0.5) * col_scale
    b_forget = 0.02 * jax.random.normal(ks[1], (DEPTH, FOX_HEADS), jnp.float32)
    w_out = jax.random.normal(ks[2], (DEPTH, MIX_WIDTH, D_MODEL), jnp.float32) * (MIX_WIDTH ** -0.5) * DN_BETA
    ln1_g = 1.0 + 0.02 * jax.random.normal(ks[3], (DEPTH, D_MODEL), jnp.float32)
    ln1_b = 0.02 * jax.random.normal(ks[4], (DEPTH, D_MODEL), jnp.float32)
    w_router = jax.random.normal(ks[5], (D_MODEL, N_EXPERTS), jnp.float32) * (D_MODEL ** -0.5)
    router_bias = 0.01 * jax.random.normal(ks[6], (N_EXPERTS,), jnp.float32)
    w_gate = jax.random.normal(ks[7], (DEPTH, N_EXPERTS, D_MODEL, D_FF), jnp.float32) * (D_MODEL ** -0.5) * DN_BETA
    w_up = jax.random.normal(ks[8], (DEPTH, N_EXPERTS, D_MODEL, D_FF), jnp.float32) * (D_MODEL ** -0.5) * DN_BETA
    w_down = jax.random.normal(ks[9], (DEPTH, N_EXPERTS, D_FF, D_MODEL), jnp.float32) * (D_FF ** -0.5) * DN_BETA
    ln2_g = 1.0 + 0.02 * jax.random.normal(ks[10], (DEPTH, D_MODEL), jnp.float32)
    ln2_b = 0.02 * jax.random.normal(ks[11], (DEPTH, D_MODEL), jnp.float32)
    x = jax.random.normal(ks[12], (BATCH, SEQ, D_MODEL), jnp.float32)
    return {"x": x, "w_in": w_in, "b_forget": b_forget, "w_out": w_out,
            "ln1_g": ln1_g, "ln1_b": ln1_b, "w_router": w_router, "router_bias": router_bias,
            "w_gate": w_gate, "w_up": w_up, "w_down": w_down, "ln2_g": ln2_g, "ln2_b": ln2_b}


def reference(x, w_in, b_forget, w_out, ln1_g, ln1_b, w_router, router_bias,
              w_gate, w_up, w_down, ln2_g, ln2_b):
    B, S, D = x.shape
    pos = jnp.arange(S, dtype=jnp.int32)
    F, Mw = FOX_WIDTH, MOBA_WIDTH
    splits = [F, 2 * F, 3 * F, 3 * F + Mw, 3 * F + 2 * Mw, 3 * F + 3 * Mw]

    def to_heads(t, n):
        return t.reshape(B, S, n, HEAD_DIM).transpose(0, 2, 1, 3)

    for l in range(DEPTH):
        proj = x @ w_in[l]
        q_f, k_f, v_f, q_m, k_m, v_m, f_logit = jnp.split(proj, splits, axis=-1)
        log_f = jax.nn.log_sigmoid((f_logit + b_forget[l]).astype(jnp.float32)).transpose(0, 2, 1)
        o_f = fox_attention(to_heads(q_f, FOX_HEADS), to_heads(k_f, FOX_HEADS),
                            to_heads(v_f, FOX_HEADS), log_f)
        o_m = moba_attention(rope(to_heads(q_m, MOBA_HEADS), pos),
                             rope(to_heads(k_m, MOBA_HEADS), pos),
                             to_heads(v_m, MOBA_HEADS))
        o = jnp.concatenate([o_f, o_m], axis=1).transpose(0, 2, 1, 3).reshape(B, S, MIX_WIDTH)
        x = layer_norm(DN_ALPHA * x + (o @ w_out[l]).astype(x.dtype), ln1_g[l], ln1_b[l])
        x_flat = x.reshape(B * S, D)
        expert_idx, gate_w = route(x_flat, w_router, router_bias)
        y = moe_ffn(x_flat, expert_idx, gate_w, w_gate[l], w_up[l], w_down[l]).reshape(B, S, D)
        x = layer_norm(DN_ALPHA * x + y, ln2_g[l], ln2_b[l])
    return x
```

```python
import functools

import jax
import jax.numpy as jnp
from jax import lax
from jax.experimental import pallas as pl
from jax.experimental.pallas import tpu as pltpu

HEAD_DIM = 128
MOBA_BLOCK = 256
MOBA_TOPK = 3
ROPE_THETA = 10000.0
N_GROUPS = 4
TOP_K = 2
LN_EPS = 1e-5
NEG_INF = -1e30
ATTN_SCALE = HEAD_DIM ** -0.5
MOE_ROWS = 256
LANES = 128
VMEM_LIMIT = 56 * 1024 * 1024

F32 = jnp.float32
BF16 = jnp.bfloat16
NT_DIMS = (((1,), (1,)), ((), ()))


def _params(*sem):
    return pltpu.CompilerParams(dimension_semantics=sem, vmem_limit_bytes=VMEM_LIMIT)


def _split3(a):
    hi = a.astype(BF16)
    r1 = a - hi.astype(F32)
    mid = r1.astype(BF16)
    lo = (r1 - mid.astype(F32)).astype(BF16)
    return hi, mid, lo


def _in_proj_kernel(x_ref, w_ref, cos_ref, sin_ref, o_ref, kbar_ref, *, tm, tn, j0, j1, j2, j3):
    acc = jnp.dot(x_ref[...], w_ref[...], preferred_element_type=F32)
    j = pl.program_id(1)

    @pl.when(j < j0)
    def _():
        o_ref[...] = (acc * ATTN_SCALE).astype(BF16)

    @pl.when(((j >= j0) & (j < j1)) | (j >= j3))
    def _():
        o_ref[...] = acc.astype(BF16)

    def roped(c):
        ch = acc[:, c * HEAD_DIM:(c + 1) * HEAD_DIM]
        return ch * cos_ref[...] + pltpu.roll(ch, HEAD_DIM // 2, 1) * sin_ref[...]

    @pl.when((j >= j1) & (j < j2))
    def _():
        for c in range(tn // HEAD_DIM):
            o_ref[:, c * HEAD_DIM:(c + 1) * HEAD_DIM] = (roped(c) * ATTN_SCALE).astype(BF16)

    @pl.when((j >= j2) & (j < j3))
    def _():
        for c in range(tn // HEAD_DIM):
            r = roped(c)
            o_ref[:, c * HEAD_DIM:(c + 1) * HEAD_DIM] = r.astype(BF16)
            kbar_ref[0, :, c * HEAD_DIM:(c + 1) * HEAD_DIM] = (
                r.reshape(tm // MOBA_BLOCK, MOBA_BLOCK, HEAD_DIM).sum(axis=1) * (1.0 / MOBA_BLOCK))


def _in_proj(xb, w, cos, sin, *, fox_w, moba_w, seq):
    n, d = xb.shape
    tm = min(1024, seq)
    tn = min(512, fox_w, moba_w)
    fq, mq = fox_w // tn, moba_w // tn
    j0, j1 = fq, 3 * fq
    j2, j3 = j1 + mq, j1 + 2 * mq
    nj = j1 + 3 * mq
    spb = seq // tm
    kern = functools.partial(_in_proj_kernel, tm=tm, tn=tn, j0=j0, j1=j1, j2=j2, j3=j3)
    return pl.pallas_call(
        kern,
        grid=(n // tm, nj),
        in_specs=[
            pl.BlockSpec((tm, d), lambda i, j: (i, 0)),
            pl.BlockSpec((d, tn), lambda i, j: (0, j)),
            pl.BlockSpec((tm, HEAD_DIM), lambda i, j: (i % spb, 0)),
            pl.BlockSpec((tm, HEAD_DIM), lambda i, j: (i % spb, 0)),
        ],
        out_specs=[
            pl.BlockSpec((tm, tn), lambda i, j: (i, j)),
            pl.BlockSpec((1, tm // MOBA_BLOCK, tn),
                         lambda i, j: (i, 0, jnp.clip(j - j2, 0, mq - 1))),
        ],
        out_shape=[
            jax.ShapeDtypeStruct((n, 3 * (fox_w + moba_w)), BF16),
            jax.ShapeDtypeStruct((n // tm, tm // MOBA_BLOCK, moba_w), F32),
        ],
        compiler_params=_params("arbitrary", "arbitrary"),
        name="in_proj",
    )(xb, w, cos, sin)


def _forget_kernel(x_ref, w_ref, b_ref, c_ref, carry_ref, *, ts):
    @pl.when(pl.program_id(1) == 0)
    def _():
        carry_ref[...] = jnp.zeros_like(carry_ref)

    f = jnp.dot(x_ref[...], w_ref[...], preferred_element_type=F32) + b_ref[...]
    lf = jnp.minimum(f, 0.0) - jnp.log1p(jnp.exp(-jnp.abs(f)))
    row = lax.broadcasted_iota(jnp.int32, (ts, ts), 0)
    col = lax.broadcasted_iota(jnp.int32, (ts, ts), 1)
    tri = jnp.where(row >= col, 1.0, 0.0).astype(BF16)
    cs = carry_ref[...]
    for part in _split3(lf):
        cs = cs + jnp.dot(tri, part, preferred_element_type=F32)
    c_ref[...] = cs
    carry_ref[...] = cs[ts - 1:ts, :]


def _forget_cumsum(xb, w_f, b_f, *, batch, seq):
    n, d = xb.shape
    ts = min(512, seq)
    spb = seq // ts
    return pl.pallas_call(
        functools.partial(_forget_kernel, ts=ts),
        grid=(batch, spb),
        in_specs=[
            pl.BlockSpec((ts, d), lambda b, s: (b * spb + s, 0)),
            pl.BlockSpec((d, LANES), lambda b, s: (0, 0)),
            pl.BlockSpec((1, LANES), lambda b, s: (0, 0)),
        ],
        out_specs=pl.BlockSpec((ts, LANES), lambda b, s: (b * spb + s, 0)),
        out_shape=jax.ShapeDtypeStruct((n, LANES), F32),
        scratch_shapes=[pltpu.VMEM((1, LANES), F32)],
        compiler_params=_params("arbitrary", "arbitrary"),
        name="forget_cumsum",
    )(xb, w_f, b_f)


def _online_softmax_step(carry, s, v):
    m, l, acc = carry
    m_new = jnp.maximum(m, s.max(axis=1, keepdims=True))
    alpha = jnp.exp(m - m_new)
    p = jnp.exp(s - m_new)
    l = alpha * l + p.sum(axis=1, keepdims=True)
    acc = alpha * acc + jnp.dot(p.astype(BF16), v, preferred_element_type=F32)
    return m_new, l, acc


def _fox_kernel(q_ref, k_ref, v_ref, c_ref, o_ref, *, t):
    qi = pl.program_id(2)
    q = q_ref[...]

    def scores(ki):
        start = pl.multiple_of(ki * t, t)
        k = k_ref[pl.ds(start, t), :]
        v = v_ref[pl.ds(start, t), :]
        s = lax.dot_general(q, k, NT_DIMS, preferred_element_type=F32) - c_ref[0, ki]
        return s, v

    def past(ki, carry):
        s, v = scores(ki)
        return _online_softmax_step(carry, s, v)

    init = (jnp.full((t, 1), -jnp.inf, F32), jnp.zeros((t, 1), F32), jnp.zeros((t, HEAD_DIM), F32))
    carry = lax.fori_loop(0, qi, past, init)
    s, v = scores(qi)
    row = lax.broadcasted_iota(jnp.int32, (t, t), 0)
    col = lax.broadcasted_iota(jnp.int32, (t, t), 1)
    _, l, acc = _online_softmax_step(carry, jnp.where(row >= col, s, NEG_INF), v)
    o_ref[...] = (acc / l).astype(BF16)


def _fox_attention(proj, c4, *, batch, seq, heads, k_col, v_col):
    n = proj.shape[0]
    t = c4.shape[-1]
    nq = seq // t
    return pl.pallas_call(
        functools.partial(_fox_kernel, t=t),
        grid=(batch, heads, nq),
        in_specs=[
            pl.BlockSpec((t, HEAD_DIM), lambda b, h, qi: (b * nq + qi, h)),
            pl.BlockSpec((seq, HEAD_DIM), lambda b, h, qi: (b, k_col + h)),
            pl.BlockSpec((seq, HEAD_DIM), lambda b, h, qi: (b, v_col + h)),
            pl.BlockSpec((1, nq, 1, t), lambda b, h, qi: (b * heads + h, 0, 0, 0)),
        ],
        out_specs=pl.BlockSpec((t, HEAD_DIM), lambda b, h, qi: (b * nq + qi, h)),
        out_shape=jax.ShapeDtypeStruct((n, heads * HEAD_DIM), BF16),
        compiler_params=_params("arbitrary", "arbitrary", "arbitrary"),
        name="fox_attention",
    )(proj, proj, proj, c4)


def _moba_kernel(q_ref, k_ref, v_ref, kbar_ref, o_ref, *, nb):
    qi = pl.program_id(2)
    q = q_ref[...]
    L = MOBA_BLOCK

    gate = jnp.zeros((L, nb), F32)
    for part in _split3(kbar_ref[0]):
        gate = gate + lax.dot_general(q, part, NT_DIMS, preferred_element_type=F32)
    col = lax.broadcasted_iota(jnp.int32, (L, nb), 1)
    valid = col < qi
    g = jnp.where(valid, gate, NEG_INF)
    rank = jnp.zeros((L, nb), jnp.int32)
    for m in range(nb):
        gm = g[:, m:m + 1]
        beats = (gm > g) | ((gm == g) & (col > m))
        rank = rank + jnp.where(beats, 1, 0)
    sel_bias = jnp.where(valid & (rank < MOBA_TOPK), 0.0, NEG_INF)

    def block(n):
        start = pl.multiple_of(n * L, L)
        k = k_ref[pl.ds(start, L), :]
        v = v_ref[pl.ds(start, L), :]
        return lax.dot_general(q, k, NT_DIMS, preferred_element_type=F32), v

    s, v = block(qi)
    row = lax.broadcasted_iota(jnp.int32, (L, L), 0)
    kcol = lax.broadcasted_iota(jnp.int32, (L, L), 1)
    init = (jnp.full((L, 1), -jnp.inf, F32), jnp.zeros((L, 1), F32), jnp.zeros((L, HEAD_DIM), F32))
    carry = _online_softmax_step(init, jnp.where(row >= kcol, s, NEG_INF), v)

    def past(n, carry):
        s, v = block(n)
        bias = jnp.sum(jnp.where(col == n, sel_bias, 0.0), axis=1, keepdims=True)
        return _online_softmax_step(carry, s + bias, v)

    _, l, acc = lax.fori_loop(0, qi, past, carry)
    o_ref[...] = (acc / l).astype(BF16)


def _moba_attention(proj, kbar, *, batch, seq, heads, q_col, k_col, v_col):
    n = proj.shape[0]
    nb = seq // MOBA_BLOCK
    return pl.pallas_call(
        functools.partial(_moba_kernel, nb=nb),
        grid=(batch, heads, nb),
        in_specs=[
            pl.BlockSpec((MOBA_BLOCK, HEAD_DIM), lambda b, h, qi: (b * nb + qi, q_col + h)),
            pl.BlockSpec((seq, HEAD_DIM), lambda b, h, qi: (b, k_col + h)),
            pl.BlockSpec((seq, HEAD_DIM), lambda b, h, qi: (b, v_col + h)),
            pl.BlockSpec((1, nb, HEAD_DIM), lambda b, h, qi: (b, 0, h)),
        ],
        out_specs=pl.BlockSpec((MOBA_BLOCK, HEAD_DIM), lambda b, h, qi: (b * nb + qi, h)),
        out_shape=jax.ShapeDtypeStruct((n, heads * HEAD_DIM), BF16),
        compiler_params=_params("arbitrary", "arbitrary", "arbitrary"),
        name="moba_attention",
    )(proj, proj, proj, kbar)


def _layer_norm_rows(z, g, b):
    mu = jnp.mean(z, axis=-1, keepdims=True)
    zc = z - mu
    var = jnp.mean(zc * zc, axis=-1, keepdims=True)
    return zc * lax.rsqrt(var + LN_EPS) * g + b


def _out_proj_kernel(of_ref, om_ref, w_ref, x_ref, g_ref, b_ref, o_ref, *, tn, fox_w, alpha):
    j = pl.program_id(1)
    acc = jnp.dot(of_ref[...], w_ref[:fox_w, :], preferred_element_type=F32)
    acc = acc + jnp.dot(om_ref[...], w_ref[fox_w:, :], preferred_element_type=F32)
    o_ref[:, pl.ds(pl.multiple_of(j * tn, tn), tn)] = alpha * x_ref[...] + acc

    @pl.when(j == pl.num_programs(1) - 1)
    def _():
        o_ref[...] = _layer_norm_rows(o_ref[...], g_ref[...], b_ref[...])


def _out_proj_ln(o_f, o_m, w, x, g, b, *, alpha):
    n, d = x.shape
    fox_w, moba_w = o_f.shape[1], o_m.shape[1]
    tm = min(512, n)
    tn = min(512, d)
    kern = functools.partial(_out_proj_kernel, tn=tn, fox_w=fox_w, alpha=alpha)
    return pl.pallas_call(
        kern,
        grid=(n // tm, d // tn),
        in_specs=[
            pl.BlockSpec((tm, fox_w), lambda i, j: (i, 0)),
            pl.BlockSpec((tm, moba_w), lambda i, j: (i, 0)),
            pl.BlockSpec((fox_w + moba_w, tn), lambda i, j: (0, j)),
            pl.BlockSpec((tm, tn), lambda i, j: (i, j)),
            pl.BlockSpec((1, d), lambda i, j: (0, 0)),
            pl.BlockSpec((1, d), lambda i, j: (0, 0)),
        ],
        out_specs=pl.BlockSpec((tm, d), lambda i, j: (i, 0)),
        out_shape=jax.ShapeDtypeStruct((n, d), F32),
        compiler_params=_params("arbitrary", "arbitrary"),
        name="out_proj_ln",
    )(o_f, o_m, w, x, g, b)


def _router_kernel(x_ref, wt_ref, bias_ref, idx_ref, wts_ref, *, n_exp):
    logits = lax.dot_general(wt_ref[...], x_ref[...], NT_DIMS, precision=lax.Precision.HIGHEST,
                             preferred_element_type=F32)
    ex = jnp.exp(logits - logits.max(axis=0, keepdims=True))
    probs = ex / ex.sum(axis=0, keepdims=True)
    sel = probs + bias_ref[...]
    per = n_exp // N_GROUPS
    gscore = []
    for gidx in range(N_GROUPS):
        r = [sel[gidx * per + a:gidx * per + a + 1, :] for a in range(per)]
        best = None
        for a in range(per):
            for b in range(a + 1, per):
                pair = r[a] + r[b]
                best = pair if best is None else jnp.maximum(best, pair)
        gscore.append(best)
    gmax = functools.reduce(jnp.maximum, gscore)
    g_idx = jnp.full_like(gmax, N_GROUPS - 1).astype(jnp.int32)
    for gidx in range(N_GROUPS - 2, -1, -1):
        g_idx = jnp.where(gscore[gidx] == gmax, gidx, g_idx)
    erow = lax.broadcasted_iota(jnp.int32, sel.shape, 0)
    masked = jnp.where(erow // per == g_idx, sel, NEG_INF)
    v1 = masked.max(axis=0, keepdims=True)
    i1 = jnp.min(jnp.where(masked == v1, erow, n_exp), axis=0, keepdims=True)
    masked2 = jnp.where(erow == i1, -jnp.inf, masked)
    v2 = masked2.max(axis=0, keepdims=True)
    i2 = jnp.min(jnp.where(masked2 == v2, erow, n_exp), axis=0, keepdims=True)
    w1 = jnp.sum(jnp.where(erow == i1, probs, 0.0), axis=0, keepdims=True)
    w2 = jnp.sum(jnp.where(erow == i2, probs, 0.0), axis=0, keepdims=True)
    tot = w1 + w2
    idx_ref[...] = jnp.concatenate([i1, i2], axis=0)
    wts_ref[...] = jnp.concatenate([w1 / tot, w2 / tot], axis=0)


def _router(x, w_router_t, bias_col):
    n, d = x.shape
    n_exp = w_router_t.shape[0]
    tm = min(512, n)
    return pl.pallas_call(
        functools.partial(_router_kernel, n_exp=n_exp),
        grid=(n // tm,),
        in_specs=[
            pl.BlockSpec((tm, d), lambda i: (i, 0)),
            pl.BlockSpec((n_exp, d), lambda i: (0, 0)),
            pl.BlockSpec((n_exp, 1), lambda i: (0, 0)),
        ],
        out_specs=[
            pl.BlockSpec((TOP_K, tm), lambda i: (0, i)),
            pl.BlockSpec((TOP_K, tm), lambda i: (0, i)),
        ],
        out_shape=[
            jax.ShapeDtypeStruct((TOP_K, n), jnp.int32),
            jax.ShapeDtypeStruct((TOP_K, n), F32),
        ],
        compiler_params=_params("arbitrary"),
        name="router",
    )(x, w_router_t, bias_col)


def _dispatch(idx, n_exp):
    k, n = idx.shape
    e = idx.reshape(-1)
    onehot = (e[:, None] == jnp.arange(n_exp, dtype=jnp.int32)[None, :]).astype(jnp.int32)
    cums = jnp.cumsum(onehot, axis=0)
    rank = jnp.take_along_axis(cums, e[:, None], axis=1)[:, 0] - 1
    counts = cums[-1]
    padded = (counts + MOE_ROWS - 1) // MOE_ROWS * MOE_ROWS
    pend = jnp.cumsum(padded)
    dest = (pend - padded)[e] + rank
    n_blocks = (k * n) // MOE_ROWS + n_exp
    tok = jnp.tile(jnp.arange(n, dtype=jnp.int32), k)
    row_tok = jnp.zeros((n_blocks * MOE_ROWS,), jnp.int32).at[dest].set(tok)
    blk_expert = jnp.minimum(
        jnp.searchsorted(pend, jnp.arange(n_blocks, dtype=jnp.int32) * MOE_ROWS, side="right"),
        n_exp - 1).astype(jnp.int32)
    n_used = (pend[-1:] // MOE_ROWS).astype(jnp.int32)
    return dest.astype(jnp.int32), row_tok, blk_expert, n_used


def _gather_rows(idx_ref, base, count, src_hbm, dst_ref, sem):
    def issue(i, c):
        r = idx_ref[base + i]
        pltpu.make_async_copy(src_hbm.at[pl.ds(r, 1), :], dst_ref.at[pl.ds(i, 1), :], sem).start()
        return c
    lax.fori_loop(0, count, issue, 0)
    pltpu.make_async_copy(src_hbm.at[pl.ds(0, count), :], dst_ref, sem).wait()


def _moe_up_kernel(blk_ref, tok_ref, nused_ref, x_hbm, wg_ref, wu_ref, h_ref, xg_ref, sem):
    b = pl.program_id(0)

    @pl.when(b < nused_ref[0])
    def _():
        _gather_rows(tok_ref, b * MOE_ROWS, MOE_ROWS, x_hbm, xg_ref, sem)
        xb = xg_ref[...].astype(BF16)
        g = jnp.dot(xb, wg_ref[0], preferred_element_type=F32)
        u = jnp.dot(xb, wu_ref[0], preferred_element_type=F32)
        h_ref[...] = (g * jax.nn.sigmoid(g) * u).astype(BF16)

    @pl.when(b >= nused_ref[0])
    def _():
        h_ref[...] = jnp.zeros_like(h_ref)


def _moe_up(x, w_gate, w_up, blk_expert, row_tok, n_used):
    n, d = x.shape
    n_blocks = blk_expert.shape[0]
    d_ff = w_gate.shape[-1]
    return pl.pallas_call(
        _moe_up_kernel,
        grid_spec=pltpu.PrefetchScalarGridSpec(
            num_scalar_prefetch=3,
            grid=(n_blocks,),
            in_specs=[
                pl.BlockSpec(memory_space=pl.ANY),
                pl.BlockSpec((1, d, d_ff), lambda b, blk, tok, nu: (blk[b], 0, 0)),
                pl.BlockSpec((1, d, d_ff), lambda b, blk, tok, nu: (blk[b], 0, 0)),
            ],
            out_specs=pl.BlockSpec((MOE_ROWS, d_ff), lambda b, blk, tok, nu: (b, 0)),
            scratch_shapes=[pltpu.VMEM((MOE_ROWS, d), F32), pltpu.SemaphoreType.DMA(())],
        ),
        out_shape=jax.ShapeDtypeStruct((n_blocks * MOE_ROWS, d_ff), BF16),
        compiler_params=_params("arbitrary"),
        name="moe_up",
    )(blk_expert, row_tok, n_used, x, w_gate, w_up)


def _moe_down_kernel(blk_ref, nused_ref, h_ref, wd_ref, y_ref):
    b = pl.program_id(0)

    @pl.when(b < nused_ref[0])
    def _():
        y_ref[...] = jnp.dot(h_ref[...], wd_ref[0], preferred_element_type=F32)

    @pl.when(b >= nused_ref[0])
    def _():
        y_ref[...] = jnp.zeros_like(y_ref)


def _moe_down(h, w_down, blk_expert, n_used):
    n_blocks = blk_expert.shape[0]
    d_ff, d = w_down.shape[1:]
    return pl.pallas_call(
        _moe_down_kernel,
        grid_spec=pltpu.PrefetchScalarGridSpec(
            num_scalar_prefetch=2,
            grid=(n_blocks,),
            in_specs=[
                pl.BlockSpec((MOE_ROWS, d_ff), lambda b, blk, nu: (b, 0)),
                pl.BlockSpec((1, d_ff, d), lambda b, blk, nu: (blk[b], 0, 0)),
            ],
            out_specs=pl.BlockSpec((MOE_ROWS, d), lambda b, blk, nu: (b, 0)),
        ),
        out_shape=jax.ShapeDtypeStruct((n_blocks * MOE_ROWS, d), F32),
        compiler_params=_params("arbitrary"),
        name="moe_down",
    )(blk_expert, n_used, h, w_down)


def _combine_kernel(dest_ref, y_hbm, x_ref, wt_ref, g_ref, b_ref, o_ref, ob_ref, y0_ref, y1_ref, sems,
                    *, tc, n_tok, alpha):
    i = pl.program_id(0)
    _gather_rows(dest_ref, i * tc, tc, y_hbm, y0_ref, sems.at[0])
    _gather_rows(dest_ref, n_tok + i * tc, tc, y_hbm, y1_ref, sems.at[1])
    moe = wt_ref[:, 0:1] * y0_ref[...] + wt_ref[:, 1:2] * y1_ref[...]
    out = _layer_norm_rows(alpha * x_ref[...] + moe, g_ref[...], b_ref[...])
    o_ref[...] = out
    ob_ref[...] = out.astype(BF16)


def _combine_ln(y, x, wts_tok, dest, g, b, *, alpha):
    n, d = x.shape
    tc = min(256, n)
    kern = functools.partial(_combine_kernel, tc=tc, n_tok=n, alpha=alpha)
    return pl.pallas_call(
        kern,
        grid_spec=pltpu.PrefetchScalarGridSpec(
            num_scalar_prefetch=1,
            grid=(n // tc,),
            in_specs=[
                pl.BlockSpec(memory_space=pl.ANY),
                pl.BlockSpec((tc, d), lambda i, dest: (i, 0)),
                pl.BlockSpec((tc, TOP_K), lambda i, dest: (i, 0)),
                pl.BlockSpec((1, d), lambda i, dest: (0, 0)),
                pl.BlockSpec((1, d), lambda i, dest: (0, 0)),
            ],
            out_specs=[
                pl.BlockSpec((tc, d), lambda i, dest: (i, 0)),
                pl.BlockSpec((tc, d), lambda i, dest: (i, 0)),
            ],
            scratch_shapes=[pltpu.VMEM((tc, d), F32), pltpu.VMEM((tc, d), F32),
                            pltpu.SemaphoreType.DMA((2,))],
        ),
        out_shape=[jax.ShapeDtypeStruct((n, d), F32), jax.ShapeDtypeStruct((n, d), BF16)],
        compiler_params=_params("arbitrary"),
        name="combine_ln",
    )(dest, y, x, wts_tok, g, b)


def _rope_tables(seq):
    half = HEAD_DIM // 2
    inv = ROPE_THETA ** (-jnp.arange(half, dtype=F32) / half)
    ang = jnp.arange(seq, dtype=jnp.int32).astype(F32)[:, None] * inv[None, :]
    cos, sin = jnp.cos(ang), jnp.sin(ang)
    return jnp.concatenate([cos, cos], axis=-1), jnp.concatenate([-sin, sin], axis=-1)


def kernel(x, w_in, b_forget, w_out, ln1_g, ln1_b, w_router, router_bias, w_gate, w_up, w_down, ln2_g, ln2_b):
    batch, seq, d = x.shape
    depth = w_in.shape[0]
    fox_heads = b_forget.shape[-1]
    fox_w = fox_heads * HEAD_DIM
    moba_w = w_out.shape[1] - fox_w
    moba_heads = moba_w // HEAD_DIM
    n_exp = w_router.shape[1]
    n = batch * seq
    alpha = (2.0 * depth) ** 0.25
    qkv_w = 3 * (fox_w + moba_w)
    hb = HEAD_DIM
    t_fox = min(512, seq)

    cos, sin = _rope_tables(seq)
    w_router_t = w_router.T
    bias_col = router_bias.reshape(n_exp, 1)
    xf = x.reshape(n, d)
    xb = xf.astype(BF16)

    for l in range(depth):
        w_qkv = w_in[l, :, :qkv_w].astype(BF16)
        w_f = jnp.pad(w_in[l, :, qkv_w:], ((0, 0), (0, LANES - fox_heads))).astype(BF16)
        b_f = jnp.pad(b_forget[l], (0, LANES - fox_heads)).reshape(1, LANES)

        proj, kbar = _in_proj(xb, w_qkv, cos, sin, fox_w=fox_w, moba_w=moba_w, seq=seq)
        c = _forget_cumsum(xb, w_f, b_f, batch=batch, seq=seq)
        c4 = (c[:, :fox_heads].reshape(batch, seq, fox_heads).transpose(0, 2, 1)
              .reshape(batch * fox_heads, seq // t_fox, 1, t_fox))
        o_f = _fox_attention(proj, c4, batch=batch, seq=seq, heads=fox_heads,
                             k_col=fox_w // hb, v_col=2 * fox_w // hb)
        o_m = _moba_attention(proj, kbar.reshape(batch, seq // MOBA_BLOCK, moba_w),
                              batch=batch, seq=seq, heads=moba_heads,
                              q_col=3 * fox_w // hb, k_col=(3 * fox_w + moba_w) // hb,
                              v_col=(3 * fox_w + 2 * moba_w) // hb)
        x1 = _out_proj_ln(o_f, o_m, w_out[l].astype(BF16), xf,
                          ln1_g[l].reshape(1, d), ln1_b[l].reshape(1, d), alpha=alpha)

        idx, wts = _router(x1, w_router_t, bias_col)
        dest, row_tok, blk_expert, n_used = _dispatch(idx, n_exp)
        h = _moe_up(x1, w_gate[l].astype(BF16), w_up[l].astype(BF16), blk_expert, row_tok, n_used)
        y = _moe_down(h, w_down[l].astype(BF16), blk_expert, n_used)
        xf, xb = _combine_ln(y, x1, wts.T, dest, ln2_g[l].reshape(1, d), ln2_b[l].reshape(1, d),
                             alpha=alpha)

    return xf.reshape(batch, seq, d)
```

```python
import functools
import math

import jax
import jax.numpy as jnp
from jax import lax
from jax.experimental import pallas as pl
from jax.experimental.pallas import tpu as pltpu

HEAD_DIM = 128
MOBA_BLOCK = 256
MOBA_TOPK = 3
ROPE_THETA = 10000.0
N_GROUPS = 4
TOP_K = 2
LN_EPS = 1e-5
NEG_INF = -1e30
LOG2E = math.log2(math.e)
Q_SCALE = HEAD_DIM ** -0.5 * LOG2E
MOE_ROWS = 256
LANES = 128
VMEM_LIMIT = 56 * 1024 * 1024
ATTN_TILE = 1024
ATTN_SUB = 256

F32 = jnp.float32
BF16 = jnp.bfloat16
NT_DIMS = (((1,), (1,)), ((), ()))


def _params(*sem):
    return pltpu.CompilerParams(dimension_semantics=sem, vmem_limit_bytes=VMEM_LIMIT)


def _split3(a):
    hi = a.astype(BF16)
    r1 = a - hi.astype(F32)
    mid = r1.astype(BF16)
    lo = (r1 - mid.astype(F32)).astype(BF16)
    return hi, mid, lo


def _in_proj_kernel(x_ref, w_ref, cos_ref, sin_ref, o_ref, kbar_ref, *, tm, tn, j0, j1, j2, j3):
    acc = jnp.dot(x_ref[...], w_ref[0].astype(BF16), preferred_element_type=F32)
    j = pl.program_id(1)

    @pl.when(j < j0)
    def _():
        o_ref[...] = (acc * Q_SCALE).astype(BF16)

    @pl.when(((j >= j0) & (j < j1)) | (j >= j3))
    def _():
        o_ref[...] = acc.astype(BF16)

    def roped(c):
        ch = acc[:, c * HEAD_DIM:(c + 1) * HEAD_DIM]
        return ch * cos_ref[...] + pltpu.roll(ch, HEAD_DIM // 2, 1) * sin_ref[...]

    @pl.when((j >= j1) & (j < j2))
    def _():
        for c in range(tn // HEAD_DIM):
            o_ref[:, c * HEAD_DIM:(c + 1) * HEAD_DIM] = (roped(c) * Q_SCALE).astype(BF16)

    @pl.when((j >= j2) & (j < j3))
    def _():
        for c in range(tn // HEAD_DIM):
            r = roped(c)
            o_ref[:, c * HEAD_DIM:(c + 1) * HEAD_DIM] = r.astype(BF16)
            kbar_ref[0, :, c * HEAD_DIM:(c + 1) * HEAD_DIM] = (
                r.reshape(tm // MOBA_BLOCK, MOBA_BLOCK, HEAD_DIM).sum(axis=1) * (1.0 / MOBA_BLOCK))


def _in_proj(xb, w_in, layer, cos, sin, *, fox_w, moba_w, seq):
    n, d = xb.shape
    tm = min(1024, seq)
    tn = min(512, fox_w, moba_w)
    fq, mq = fox_w // tn, moba_w // tn
    j0, j1 = fq, 3 * fq
    j2, j3 = j1 + mq, j1 + 2 * mq
    nj = j1 + 3 * mq
    spb = seq // tm
    kern = functools.partial(_in_proj_kernel, tm=tm, tn=tn, j0=j0, j1=j1, j2=j2, j3=j3)
    return pl.pallas_call(
        kern,
        grid=(n // tm, nj),
        in_specs=[
            pl.BlockSpec((tm, d), lambda i, j: (i, 0)),
            pl.BlockSpec((1, d, tn), lambda i, j: (layer, 0, j)),
            pl.BlockSpec((tm, HEAD_DIM), lambda i, j: (i % spb, 0)),
            pl.BlockSpec((tm, HEAD_DIM), lambda i, j: (i % spb, 0)),
        ],
        out_specs=[
            pl.BlockSpec((tm, tn), lambda i, j: (i, j)),
            pl.BlockSpec((1, tm // MOBA_BLOCK, tn),
                         lambda i, j: (i, 0, jnp.clip(j - j2, 0, mq - 1))),
        ],
        out_shape=[
            jax.ShapeDtypeStruct((n, 3 * (fox_w + moba_w)), BF16),
            jax.ShapeDtypeStruct((n // tm, tm // MOBA_BLOCK, moba_w), F32),
        ],
        compiler_params=_params("arbitrary", "arbitrary"),
        name="in_proj",
    )(xb, w_in, cos, sin)


def _forget_kernel(x_ref, w_ref, b_ref, cp_ref, carry_ref, *, ts, heads):
    @pl.when(pl.program_id(1) == 0)
    def _():
        carry_ref[...] = jnp.zeros_like(carry_ref)

    f = jnp.dot(x_ref[...], w_ref[...], preferred_element_type=F32) + b_ref[...]
    lf = jnp.minimum(f, 0.0) - jnp.log1p(jnp.exp(-jnp.abs(f)))
    row = lax.broadcasted_iota(jnp.int32, (ts, ts), 0)
    col = lax.broadcasted_iota(jnp.int32, (ts, ts), 1)
    tri = jnp.where(row >= col, 1.0, 0.0).astype(BF16)
    cs = carry_ref[...]
    for part in _split3(lf):
        cs = cs + jnp.dot(tri, part, preferred_element_type=F32)
    carry_ref[...] = cs[ts - 1:ts, :]

    lane = lax.broadcasted_iota(jnp.int32, (ts, LANES), 1)
    neg = jnp.where(lane < heads, -LOG2E * cs, 0.0)
    erow = lax.broadcasted_iota(jnp.int32, (LANES, LANES), 0)
    ecol = lax.broadcasted_iota(jnp.int32, (LANES, LANES), 1)
    out = jnp.zeros((ts, LANES), F32)
    for j, part in enumerate(_split3(neg)):
        spread = jnp.where(ecol == 3 * erow + j, 1.0, 0.0).astype(BF16)
        out = out + jnp.dot(part, spread, preferred_element_type=F32)
    cp_ref[...] = out.astype(BF16)


def _forget_cumsum(xb, w_f, b_f, *, batch, seq, heads):
    n, d = xb.shape
    assert 3 * heads <= LANES
    ts = min(512, seq)
    spb = seq // ts
    return pl.pallas_call(
        functools.partial(_forget_kernel, ts=ts, heads=heads),
        grid=(batch, spb),
        in_specs=[
            pl.BlockSpec((ts, d), lambda b, s: (b * spb + s, 0)),
            pl.BlockSpec((d, LANES), lambda b, s: (0, 0)),
            pl.BlockSpec((1, LANES), lambda b, s: (0, 0)),
        ],
        out_specs=pl.BlockSpec((ts, LANES), lambda b, s: (b * spb + s, 0)),
        out_shape=jax.ShapeDtypeStruct((n, LANES), BF16),
        scratch_shapes=[pltpu.VMEM((1, LANES), F32)],
        compiler_params=_params("arbitrary", "arbitrary"),
        name="forget_cumsum",
    )(xb, w_f, b_f)


def _online_softmax_step(carry, s, v):
    m, l, acc = carry
    m_new = jnp.maximum(m, s.max(axis=1, keepdims=True))
    alpha = jnp.exp2(m - m_new)
    p = jnp.exp2(s - m_new)
    l = alpha * l + p.sum(axis=1, keepdims=True)
    acc = alpha * acc + jnp.dot(p.astype(BF16), v, preferred_element_type=F32)
    return m_new, l, acc


def _flash_attention(q_aug, kaug_ref, v_ref, o_ref, qi, *, t, sub):
    n_sub = t // sub
    q_subs = [q_aug[r * sub:(r + 1) * sub] for r in range(n_sub)]

    def past(ki, carry):
        start = pl.multiple_of(ki * t, t)
        k = kaug_ref[pl.ds(start, t), :]
        v = v_ref[pl.ds(start, t), :]
        return tuple(
            _online_softmax_step(carry[r], lax.dot_general(q_subs[r], k, NT_DIMS, preferred_element_type=F32), v)
            for r in range(n_sub))

    init = tuple((jnp.full((sub, 1), -jnp.inf, F32), jnp.zeros((sub, 1), F32), jnp.zeros((sub, HEAD_DIM), F32))
                 for _ in range(n_sub))
    carry = lax.fori_loop(0, qi, past, init)

    start = pl.multiple_of(qi * t, t)
    for r in range(n_sub):
        width = (r + 1) * sub
        k = kaug_ref[pl.ds(start, width), :]
        v = v_ref[pl.ds(start, width), :]
        s = lax.dot_general(q_subs[r], k, NT_DIMS, preferred_element_type=F32)
        row = lax.broadcasted_iota(jnp.int32, (sub, width), 0) + r * sub
        col = lax.broadcasted_iota(jnp.int32, (sub, width), 1)
        _, l, acc = _online_softmax_step(carry[r], jnp.where(row >= col, s, NEG_INF), v)
        o_ref[r * sub:(r + 1) * sub, :] = (acc / l).astype(BF16)


def _fox_kernel(q_ref, k_ref, v_ref, cp_ref, o_ref, kaug_ref, *, t, sub):
    h = pl.program_id(1)
    qi = pl.program_id(2)

    @pl.when(qi == 0)
    def _():
        kaug_ref[:, :HEAD_DIM] = k_ref[...]
        lane = lax.broadcasted_iota(jnp.int32, cp_ref.shape, 1)
        mine = (lane >= 3 * h) & (lane < 3 * h + 3)
        kaug_ref[:, HEAD_DIM:] = jnp.where(mine, cp_ref[...], jnp.zeros_like(cp_ref))

    lane = lax.broadcasted_iota(jnp.int32, (t, LANES), 1)
    ones = jnp.where((lane >= 3 * h) & (lane < 3 * h + 3), 1.0, 0.0).astype(BF16)
    q_aug = jnp.concatenate([q_ref[...], ones], axis=1)
    _flash_attention(q_aug, kaug_ref, v_ref, o_ref, qi, t=t, sub=sub)


def _fox_attention(proj, cp, *, batch, seq, heads, k_col, v_col):
    n = proj.shape[0]
    t = min(ATTN_TILE, seq)
    sub = min(ATTN_SUB, t)
    nq = seq // t
    return pl.pallas_call(
        functools.partial(_fox_kernel, t=t, sub=sub),
        grid=(batch, heads, nq),
        in_specs=[
            pl.BlockSpec((t, HEAD_DIM), lambda b, h, qi: (b * nq + qi, h)),
            pl.BlockSpec((seq, HEAD_DIM), lambda b, h, qi: (b, k_col + h)),
            pl.BlockSpec((seq, HEAD_DIM), lambda b, h, qi: (b, v_col + h)),
            pl.BlockSpec((seq, LANES), lambda b, h, qi: (b, 0)),
        ],
        out_specs=pl.BlockSpec((t, HEAD_DIM), lambda b, h, qi: (b * nq + qi, h)),
        out_shape=jax.ShapeDtypeStruct((n, heads * HEAD_DIM), BF16),
        scratch_shapes=[pltpu.VMEM((seq, 2 * HEAD_DIM), BF16)],
        compiler_params=_params("arbitrary", "arbitrary", "arbitrary"),
        name="fox_attention",
    )(proj, proj, proj, cp)


def _moba_kernel(q_ref, k_ref, v_ref, kbar_ref, o_ref, kaug_ref, *, t, sub, nb):
    qi = pl.program_id(2)
    seq = k_ref.shape[0]

    @pl.when(qi == 0)
    def _():
        kaug_ref[:, :HEAD_DIM] = k_ref[...]
        r = lax.broadcasted_iota(jnp.int32, (seq, LANES), 0)
        c = lax.broadcasted_iota(jnp.int32, (seq, LANES), 1)
        kaug_ref[:, HEAD_DIM:] = jnp.where(r // MOBA_BLOCK == c, 1.0, 0.0).astype(BF16)

    q = q_ref[...]
    gate = jnp.zeros((nb, t), F32)
    for part in _split3(kbar_ref[0]):
        gate = gate + lax.dot_general(part, q, NT_DIMS, preferred_element_type=F32)
    blk = lax.broadcasted_iota(jnp.int32, (nb, t), 0)
    own = qi * (t // MOBA_BLOCK) + lax.broadcasted_iota(jnp.int32, (nb, t), 1) // MOBA_BLOCK
    valid = blk < own
    g = jnp.where(valid, gate, NEG_INF)
    rank = jnp.zeros((nb, t), jnp.int32)
    for m in range(nb):
        gm = g[m:m + 1, :]
        beats = (gm > g) | ((gm == g) & (blk > m))
        rank = rank + jnp.where(beats, 1, 0)
    allowed = (valid & (rank < MOBA_TOPK)) | (blk == own)
    bias_t = jnp.where(allowed, 0.0, NEG_INF)
    bias = jnp.concatenate([bias_t, jnp.zeros((LANES - nb, t), F32)], axis=0).T
    q_aug = jnp.concatenate([q, bias.astype(BF16)], axis=1)
    _flash_attention(q_aug, kaug_ref, v_ref, o_ref, qi, t=t, sub=sub)


def _moba_attention(proj, kbar, *, batch, seq, heads, q_col, k_col, v_col):
    n = proj.shape[0]
    nb = seq // MOBA_BLOCK
    assert nb <= LANES
    t = min(ATTN_TILE, seq)
    nq = seq // t
    return pl.pallas_call(
        functools.partial(_moba_kernel, t=t, sub=MOBA_BLOCK, nb=nb),
        grid=(batch, heads, nq),
        in_specs=[
            pl.BlockSpec((t, HEAD_DIM), lambda b, h, qi: (b * nq + qi, q_col + h)),
            pl.BlockSpec((seq, HEAD_DIM), lambda b, h, qi: (b, k_col + h)),
            pl.BlockSpec((seq, HEAD_DIM), lambda b, h, qi: (b, v_col + h)),
            pl.BlockSpec((1, nb, HEAD_DIM), lambda b, h, qi: (b, 0, h)),
        ],
        out_specs=pl.BlockSpec((t, HEAD_DIM), lambda b, h, qi: (b * nq + qi, h)),
        out_shape=jax.ShapeDtypeStruct((n, heads * HEAD_DIM), BF16),
        scratch_shapes=[pltpu.VMEM((seq, 2 * HEAD_DIM), BF16)],
        compiler_params=_params("arbitrary", "arbitrary", "arbitrary"),
        name="moba_attention",
    )(proj, proj, proj, kbar)


def _layer_norm_rows(z, g, b):
    mu = jnp.mean(z, axis=-1, keepdims=True)
    zc = z - mu
    var = jnp.mean(zc * zc, axis=-1, keepdims=True)
    return zc * lax.rsqrt(var + LN_EPS) * g + b


def _out_proj_kernel(of_ref, om_ref, w_ref, x_ref, g_ref, b_ref, o_ref, *, tn, fox_w, alpha):
    j = pl.program_id(1)
    acc = jnp.dot(of_ref[...], w_ref[:fox_w, :], preferred_element_type=F32)
    acc = acc + jnp.dot(om_ref[...], w_ref[fox_w:, :], preferred_element_type=F32)
    o_ref[:, pl.ds(pl.multiple_of(j * tn, tn), tn)] = alpha * x_ref[...] + acc

    @pl.when(j == pl.num_programs(1) - 1)
    def _():
        o_ref[...] = _layer_norm_rows(o_ref[...], g_ref[...], b_ref[...])


def _out_proj_ln(o_f, o_m, w, x, g, b, *, alpha):
    n, d = x.shape
    fox_w, moba_w = o_f.shape[1], o_m.shape[1]
    tm = min(512, n)
    tn = min(512, d)
    kern = functools.partial(_out_proj_kernel, tn=tn, fox_w=fox_w, alpha=alpha)
    return pl.pallas_call(
        kern,
        grid=(n // tm, d // tn),
        in_specs=[
            pl.BlockSpec((tm, fox_w), lambda i, j: (i, 0)),
            pl.BlockSpec((tm, moba_w), lambda i, j: (i, 0)),
            pl.BlockSpec((fox_w + moba_w, tn), lambda i, j: (0, j)),
            pl.BlockSpec((tm, tn), lambda i, j: (i, j)),
            pl.BlockSpec((1, d), lambda i, j: (0, 0)),
            pl.BlockSpec((1, d), lambda i, j: (0, 0)),
        ],
        out_specs=pl.BlockSpec((tm, d), lambda i, j: (i, 0)),
        out_shape=jax.ShapeDtypeStruct((n, d), F32),
        compiler_params=_params("arbitrary", "arbitrary"),
        name="out_proj_ln",
    )(o_f, o_m, w, x, g, b)


def _router_kernel(x_ref, wt_ref, bias_ref, idx_ref, wts_ref, *, n_exp):
    logits = lax.dot_general(wt_ref[...], x_ref[...], NT_DIMS, precision=lax.Precision.HIGHEST,
                             preferred_element_type=F32)
    ex = jnp.exp(logits - logits.max(axis=0, keepdims=True))
    probs = ex / ex.sum(axis=0, keepdims=True)
    sel = probs + bias_ref[...]
    per = n_exp // N_GROUPS
    gscore = []
    for gidx in range(N_GROUPS):
        r = [sel[gidx * per + a:gidx * per + a + 1, :] for a in range(per)]
        best = None
        for a in range(per):
            for b in range(a + 1, per):
                pair = r[a] + r[b]
                best = pair if best is None else jnp.maximum(best, pair)
        gscore.append(best)
    gmax = functools.reduce(jnp.maximum, gscore)
    g_idx = jnp.full_like(gmax, N_GROUPS - 1).astype(jnp.int32)
    for gidx in range(N_GROUPS - 2, -1, -1):
        g_idx = jnp.where(gscore[gidx] == gmax, gidx, g_idx)
    erow = lax.broadcasted_iota(jnp.int32, sel.shape, 0)
    masked = jnp.where(erow // per == g_idx, sel, NEG_INF)
    v1 = masked.max(axis=0, keepdims=True)
    i1 = jnp.min(jnp.where(masked == v1, erow, n_exp), axis=0, keepdims=True)
    masked2 = jnp.where(erow == i1, -jnp.inf, masked)
    v2 = masked2.max(axis=0, keepdims=True)
    i2 = jnp.min(jnp.where(masked2 == v2, erow, n_exp), axis=0, keepdims=True)
    w1 = jnp.sum(jnp.where(erow == i1, probs, 0.0), axis=0, keepdims=True)
    w2 = jnp.sum(jnp.where(erow == i2, probs, 0.0), axis=0, keepdims=True)
    tot = w1 + w2
    idx_ref[...] = jnp.concatenate([i1, i2], axis=0)
    wts_ref[...] = jnp.concatenate([w1 / tot, w2 / tot], axis=0)


def _router(x, w_router_t, bias_col):
    n, d = x.shape
    n_exp = w_router_t.shape[0]
    tm = min(512, n)
    return pl.pallas_call(
        functools.partial(_router_kernel, n_exp=n_exp),
        grid=(n // tm,),
        in_specs=[
            pl.BlockSpec((tm, d), lambda i: (i, 0)),
            pl.BlockSpec((n_exp, d), lambda i: (0, 0)),
            pl.BlockSpec((n_exp, 1), lambda i: (0, 0)),
        ],
        out_specs=[
            pl.BlockSpec((TOP_K, tm), lambda i: (0, i)),
            pl.BlockSpec((TOP_K, tm), lambda i: (0, i)),
        ],
        out_shape=[
            jax.ShapeDtypeStruct((TOP_K, n), jnp.int32),
            jax.ShapeDtypeStruct((TOP_K, n), F32),
        ],
        compiler_params=_params("arbitrary"),
        name="router",
    )(x, w_router_t, bias_col)


def _dispatch(idx, n_exp):
    k, n = idx.shape
    e = idx.reshape(-1)
    onehot = (e[:, None] == jnp.arange(n_exp, dtype=jnp.int32)[None, :]).astype(jnp.int32)
    cums = jnp.cumsum(onehot, axis=0)
    rank = jnp.take_along_axis(cums, e[:, None], axis=1)[:, 0] - 1
    counts = cums[-1]
    padded = (counts + MOE_ROWS - 1) // MOE_ROWS * MOE_ROWS
    pend = jnp.cumsum(padded)
    dest = (pend - padded)[e] + rank
    n_blocks = (k * n) // MOE_ROWS + n_exp
    tok = jnp.tile(jnp.arange(n, dtype=jnp.int32), k)
    row_tok = jnp.zeros((n_blocks * MOE_ROWS,), jnp.int32).at[dest].set(tok)
    blk_start = jnp.arange(n_blocks, dtype=jnp.int32) * MOE_ROWS
    blk_expert = jnp.minimum(
        jnp.sum((pend[None, :] <= blk_start[:, None]).astype(jnp.int32), axis=1), n_exp - 1)
    n_used = (pend[-1:] // MOE_ROWS).astype(jnp.int32)
    return dest.astype(jnp.int32), row_tok, blk_expert, n_used


def _gather_rows(idx_ref, base, count, src_hbm, dst_ref, sem):
    def issue(i, c):
        r = idx_ref[base + i]
        pltpu.make_async_copy(src_hbm.at[pl.ds(r, 1), :], dst_ref.at[pl.ds(i, 1), :], sem).start()
        return c
    lax.fori_loop(0, count, issue, 0)
    pltpu.make_async_copy(src_hbm.at[pl.ds(0, count), :], dst_ref, sem).wait()


def _moe_up_kernel(blk_ref, tok_ref, nused_ref, x_hbm, wg_ref, wu_ref, h_ref, xg_ref, sem):
    b = pl.program_id(0)

    @pl.when(b < nused_ref[0])
    def _():
        _gather_rows(tok_ref, b * MOE_ROWS, MOE_ROWS, x_hbm, xg_ref, sem)
        xb = xg_ref[...].astype(BF16)
        g = jnp.dot(xb, wg_ref[0], preferred_element_type=F32)
        u = jnp.dot(xb, wu_ref[0], preferred_element_type=F32)
        h_ref[...] = (g * jax.nn.sigmoid(g) * u).astype(BF16)

    @pl.when(b >= nused_ref[0])
    def _():
        h_ref[...] = jnp.zeros_like(h_ref)


def _moe_up(x, w_gate, w_up, blk_expert, row_tok, n_used):
    n, d = x.shape
    n_blocks = blk_expert.shape[0]
    d_ff = w_gate.shape[-1]
    return pl.pallas_call(
        _moe_up_kernel,
        grid_spec=pltpu.PrefetchScalarGridSpec(
            num_scalar_prefetch=3,
            grid=(n_blocks,),
            in_specs=[
                pl.BlockSpec(memory_space=pl.ANY),
                pl.BlockSpec((1, d, d_ff), lambda b, blk, tok, nu: (blk[b], 0, 0)),
                pl.BlockSpec((1, d, d_ff), lambda b, blk, tok, nu: (blk[b], 0, 0)),
            ],
            out_specs=pl.BlockSpec((MOE_ROWS, d_ff), lambda b, blk, tok, nu: (b, 0)),
            scratch_shapes=[pltpu.VMEM((MOE_ROWS, d), F32), pltpu.SemaphoreType.DMA(())],
        ),
        out_shape=jax.ShapeDtypeStruct((n_blocks * MOE_ROWS, d_ff), BF16),
        compiler_params=_params("arbitrary"),
        name="moe_up",
    )(blk_expert, row_tok, n_used, x, w_gate, w_up)


def _moe_down_kernel(blk_ref, nused_ref, h_ref, wd_ref, y_ref):
    b = pl.program_id(0)

    @pl.when(b < nused_ref[0])
    def _():
        y_ref[...] = jnp.dot(h_ref[...], wd_ref[0], preferred_element_type=F32)

    @pl.when(b >= nused_ref[0])
    def _():
        y_ref[...] = jnp.zeros_like(y_ref)


def _moe_down(h, w_down, blk_expert, n_used):
    n_blocks = blk_expert.shape[0]
    d_ff, d = w_down.shape[1:]
    return pl.pallas_call(
        _moe_down_kernel,
        grid_spec=pltpu.PrefetchScalarGridSpec(
            num_scalar_prefetch=2,
            grid=(n_blocks,),
            in_specs=[
                pl.BlockSpec((MOE_ROWS, d_ff), lambda b, blk, nu: (b, 0)),
                pl.BlockSpec((1, d_ff, d), lambda b, blk, nu: (blk[b], 0, 0)),
            ],
            out_specs=pl.BlockSpec((MOE_ROWS, d), lambda b, blk, nu: (b, 0)),
        ),
        out_shape=jax.ShapeDtypeStruct((n_blocks * MOE_ROWS, d), F32),
        compiler_params=_params("arbitrary"),
        name="moe_down",
    )(blk_expert, n_used, h, w_down)


def _combine_kernel(dest_ref, y_hbm, x_ref, wt_ref, g_ref, b_ref, o_ref, ob_ref, y0_ref, y1_ref, sems,
                    *, tc, n_tok, alpha):
    i = pl.program_id(0)
    _gather_rows(dest_ref, i * tc, tc, y_hbm, y0_ref, sems.at[0])
    _gather_rows(dest_ref, n_tok + i * tc, tc, y_hbm, y1_ref, sems.at[1])
    moe = wt_ref[:, 0:1] * y0_ref[...] + wt_ref[:, 1:2] * y1_ref[...]
    out = _layer_norm_rows(alpha * x_ref[...] + moe, g_ref[...], b_ref[...])
    o_ref[...] = out
    ob_ref[...] = out.astype(BF16)


def _combine_ln(y, x, wts_tok, dest, g, b, *, alpha):
    n, d = x.shape
    tc = min(256, n)
    kern = functools.partial(_combine_kernel, tc=tc, n_tok=n, alpha=alpha)
    return pl.pallas_call(
        kern,
        grid_spec=pltpu.PrefetchScalarGridSpec(
            num_scalar_prefetch=1,
            grid=(n // tc,),
            in_specs=[
                pl.BlockSpec(memory_space=pl.ANY),
                pl.BlockSpec((tc, d), lambda i, dest: (i, 0)),
                pl.BlockSpec((tc, TOP_K), lambda i, dest: (i, 0)),
                pl.BlockSpec((1, d), lambda i, dest: (0, 0)),
                pl.BlockSpec((1, d), lambda i, dest: (0, 0)),
            ],
            out_specs=[
                pl.BlockSpec((tc, d), lambda i, dest: (i, 0)),
                pl.BlockSpec((tc, d), lambda i, dest: (i, 0)),
            ],
            scratch_shapes=[pltpu.VMEM((tc, d), F32), pltpu.VMEM((tc, d), F32),
                            pltpu.SemaphoreType.DMA((2,))],
        ),
        out_shape=[jax.ShapeDtypeStruct((n, d), F32), jax.ShapeDtypeStruct((n, d), BF16)],
        compiler_params=_params("arbitrary"),
        name="combine_ln",
    )(dest, y, x, wts_tok, g, b)


def _rope_tables(seq):
    half = HEAD_DIM // 2
    inv = ROPE_THETA ** (-jnp.arange(half, dtype=F32) / half)
    ang = jnp.arange(seq, dtype=jnp.int32).astype(F32)[:, None] * inv[None, :]
    cos, sin = jnp.cos(ang), jnp.sin(ang)
    return jnp.concatenate([cos, cos], axis=-1), jnp.concatenate([-sin, sin], axis=-1)


def kernel(x, w_in, b_forget, w_out, ln1_g, ln1_b, w_router, router_bias, w_gate, w_up, w_down, ln2_g, ln2_b):
    batch, seq, d = x.shape
    depth = w_in.shape[0]
    fox_heads = b_forget.shape[-1]
    fox_w = fox_heads * HEAD_DIM
    moba_w = w_out.shape[1] - fox_w
    moba_heads = moba_w // HEAD_DIM
    n_exp = w_router.shape[1]
    n = batch * seq
    alpha = (2.0 * depth) ** 0.25
    qkv_w = 3 * (fox_w + moba_w)
    hb = HEAD_DIM

    cos, sin = _rope_tables(seq)
    w_router_t = w_router.T
    bias_col = router_bias.reshape(n_exp, 1)
    xf = x.reshape(n, d)
    xb = xf.astype(BF16)

    for l in range(depth):
        w_f = jnp.pad(w_in[l, :, qkv_w:], ((0, 0), (0, LANES - fox_heads))).astype(BF16)
        b_f = jnp.pad(b_forget[l], (0, LANES - fox_heads)).reshape(1, LANES)

        proj, kbar = _in_proj(xb, w_in, l, cos, sin, fox_w=fox_w, moba_w=moba_w, seq=seq)
        cp = _forget_cumsum(xb, w_f, b_f, batch=batch, seq=seq, heads=fox_heads)
        o_f = _fox_attention(proj, cp, batch=batch, seq=seq, heads=fox_heads,
                             k_col=fox_w // hb, v_col=2 * fox_w // hb)
        o_m = _moba_attention(proj, kbar.reshape(batch, seq // MOBA_BLOCK, moba_w),
                              batch=batch, seq=seq, heads=moba_heads,
                              q_col=3 * fox_w // hb, k_col=(3 * fox_w + moba_w) // hb,
                              v_col=(3 * fox_w + 2 * moba_w) // hb)
        x1 = _out_proj_ln(o_f, o_m, w_out[l].astype(BF16), xf,
                          ln1_g[l].reshape(1, d), ln1_b[l].reshape(1, d), alpha=alpha)

        idx, wts = _router(x1, w_router_t, bias_col)
        dest, row_tok, blk_expert, n_used = _dispatch(idx, n_exp)
        h = _moe_up(x1, w_gate[l].astype(BF16), w_up[l].astype(BF16), blk_expert, row_tok, n_used)
        y = _moe_down(h, w_down[l].astype(BF16), blk_expert, n_used)
        xf, xb = _combine_ln(y, x1, wts.T, dest, ln2_g[l].reshape(1, d), ln2_b[l].reshape(1, d),
                             alpha=alpha)

    return xf.reshape(batch, seq, d)
```

```python
import functools
import math

import jax
import jax.numpy as jnp
from jax import lax
from jax.experimental import pallas as pl
from jax.experimental.pallas import tpu as pltpu

HEAD_DIM = 128
MOBA_BLOCK = 256
MOBA_TOPK = 3
ROPE_THETA = 10000.0
N_GROUPS = 4
TOP_K = 2
LN_EPS = 1e-5
NEG_INF = -1e30
LOG2E = math.log2(math.e)
Q_SCALE = HEAD_DIM ** -0.5 * LOG2E
MOE_ROWS = 256
LANES = 128
VMEM_LIMIT = 56 * 1024 * 1024
ATTN_TILE = 1024
ATTN_SUB = 256

F32 = jnp.float32
BF16 = jnp.bfloat16
NT_DIMS = (((1,), (1,)), ((), ()))


def _params(*sem):
    return pltpu.CompilerParams(dimension_semantics=sem, vmem_limit_bytes=VMEM_LIMIT)


def _split3(a):
    hi = a.astype(BF16)
    r1 = a - hi.astype(F32)
    mid = r1.astype(BF16)
    lo = (r1 - mid.astype(F32)).astype(BF16)
    return hi, mid, lo


def _in_proj_kernel(x_ref, w_ref, cos_ref, sin_ref, o_ref, kbar_ref, *, tm, tn, j0, j1, j2, j3):
    acc = lax.dot_general(x_ref[...], w_ref[0].astype(BF16), NT_DIMS, preferred_element_type=F32)
    j = pl.program_id(1)

    @pl.when(j < j0)
    def _():
        o_ref[...] = (acc * Q_SCALE).astype(BF16)

    @pl.when(((j >= j0) & (j < j1)) | (j >= j3))
    def _():
        o_ref[...] = acc.astype(BF16)

    def roped(c):
        ch = acc[:, c * HEAD_DIM:(c + 1) * HEAD_DIM]
        return ch * cos_ref[...] + pltpu.roll(ch, HEAD_DIM // 2, 1) * sin_ref[...]

    @pl.when((j >= j1) & (j < j2))
    def _():
        for c in range(tn // HEAD_DIM):
            o_ref[:, c * HEAD_DIM:(c + 1) * HEAD_DIM] = (roped(c) * Q_SCALE).astype(BF16)

    @pl.when((j >= j2) & (j < j3))
    def _():
        for c in range(tn // HEAD_DIM):
            r = roped(c)
            o_ref[:, c * HEAD_DIM:(c + 1) * HEAD_DIM] = r.astype(BF16)
            kbar_ref[0, :, c * HEAD_DIM:(c + 1) * HEAD_DIM] = (
                r.reshape(tm // MOBA_BLOCK, MOBA_BLOCK, HEAD_DIM).sum(axis=1) * (1.0 / MOBA_BLOCK))


def _in_proj(xb, w_in_t, layer, cos, sin, *, fox_w, moba_w, seq):
    n, d = xb.shape
    tm = min(1024, seq)
    tn = min(512, fox_w, moba_w)
    fq, mq = fox_w // tn, moba_w // tn
    j0, j1 = fq, 3 * fq
    j2, j3 = j1 + mq, j1 + 2 * mq
    nj = j1 + 3 * mq
    spb = seq // tm
    kern = functools.partial(_in_proj_kernel, tm=tm, tn=tn, j0=j0, j1=j1, j2=j2, j3=j3)
    return pl.pallas_call(
        kern,
        grid=(n // tm, nj),
        in_specs=[
            pl.BlockSpec((tm, d), lambda i, j: (i, 0)),
            pl.BlockSpec((1, tn, d), lambda i, j: (layer, j, 0)),
            pl.BlockSpec((tm, HEAD_DIM), lambda i, j: (i % spb, 0)),
            pl.BlockSpec((tm, HEAD_DIM), lambda i, j: (i % spb, 0)),
        ],
        out_specs=[
            pl.BlockSpec((tm, tn), lambda i, j: (i, j)),
            pl.BlockSpec((1, tm // MOBA_BLOCK, tn),
                         lambda i, j: (i, 0, jnp.clip(j - j2, 0, mq - 1))),
        ],
        out_shape=[
            jax.ShapeDtypeStruct((n, 3 * (fox_w + moba_w)), BF16),
            jax.ShapeDtypeStruct((n // tm, tm // MOBA_BLOCK, moba_w), F32),
        ],
        compiler_params=_params("arbitrary", "arbitrary"),
        name="in_proj",
    )(xb, w_in_t, cos, sin)


def _forget_kernel(x_ref, w_ref, b_ref, cp_ref, carry_ref, *, ts, heads):
    @pl.when(pl.program_id(1) == 0)
    def _():
        carry_ref[...] = jnp.zeros_like(carry_ref)

    f = lax.dot_general(x_ref[...], w_ref[...].astype(BF16), NT_DIMS, preferred_element_type=F32) + b_ref[...]
    lf = jnp.minimum(f, 0.0) - jnp.log1p(jnp.exp(-jnp.abs(f)))
    row = lax.broadcasted_iota(jnp.int32, (ts, ts), 0)
    col = lax.broadcasted_iota(jnp.int32, (ts, ts), 1)
    tri = jnp.where(row >= col, 1.0, 0.0).astype(BF16)
    cs = carry_ref[...]
    for part in _split3(lf):
        cs = cs + jnp.dot(tri, part, preferred_element_type=F32)
    carry_ref[...] = cs[ts - 1:ts, :]

    lane = lax.broadcasted_iota(jnp.int32, (ts, LANES), 1)
    neg = jnp.where(lane < heads, -LOG2E * cs, 0.0)
    erow = lax.broadcasted_iota(jnp.int32, (LANES, LANES), 0)
    ecol = lax.broadcasted_iota(jnp.int32, (LANES, LANES), 1)
    out = jnp.zeros((ts, LANES), F32)
    for j, part in enumerate(_split3(neg)):
        spread = jnp.where(ecol == 3 * erow + j, 1.0, 0.0).astype(BF16)
        out = out + jnp.dot(part, spread, preferred_element_type=F32)
    cp_ref[...] = out.astype(BF16)


def _forget_cumsum(xb, w_f, b_f, *, batch, seq, heads):
    n, d = xb.shape
    assert 3 * heads <= LANES
    ts = min(512, seq)
    spb = seq // ts
    return pl.pallas_call(
        functools.partial(_forget_kernel, ts=ts, heads=heads),
        grid=(batch, spb),
        in_specs=[
            pl.BlockSpec((ts, d), lambda b, s: (b * spb + s, 0)),
            pl.BlockSpec((LANES, d), lambda b, s: (0, 0)),
            pl.BlockSpec((1, LANES), lambda b, s: (0, 0)),
        ],
        out_specs=pl.BlockSpec((ts, LANES), lambda b, s: (b * spb + s, 0)),
        out_shape=jax.ShapeDtypeStruct((n, LANES), BF16),
        scratch_shapes=[pltpu.VMEM((1, LANES), F32)],
        compiler_params=_params("arbitrary", "arbitrary"),
        name="forget_cumsum",
    )(xb, w_f, b_f)


def _online_softmax_step(carry, s, v):
    m, l, acc = carry
    m_new = jnp.maximum(m, s.max(axis=1, keepdims=True))
    alpha = jnp.exp2(m - m_new)
    p = jnp.exp2(s - m_new)
    l = alpha * l + p.sum(axis=1, keepdims=True)
    acc = alpha * acc + jnp.dot(p.astype(BF16), v, preferred_element_type=F32)
    return m_new, l, acc


def _flash_attention(q_aug, kaug_ref, v_ref, o_ref, qi, *, t, sub):
    n_sub = t // sub
    q_subs = [q_aug[r * sub:(r + 1) * sub] for r in range(n_sub)]

    def past(ki, carry):
        start = pl.multiple_of(ki * t, t)
        k = kaug_ref[pl.ds(start, t), :]
        v = v_ref[pl.ds(start, t), :]
        return tuple(
            _online_softmax_step(carry[r], lax.dot_general(q_subs[r], k, NT_DIMS, preferred_element_type=F32), v)
            for r in range(n_sub))

    init = tuple((jnp.full((sub, 1), -jnp.inf, F32), jnp.zeros((sub, 1), F32), jnp.zeros((sub, HEAD_DIM), F32))
                 for _ in range(n_sub))
    carry = lax.fori_loop(0, qi, past, init)

    start = pl.multiple_of(qi * t, t)
    for r in range(n_sub):
        width = (r + 1) * sub
        k = kaug_ref[pl.ds(start, width), :]
        v = v_ref[pl.ds(start, width), :]
        s = lax.dot_general(q_subs[r], k, NT_DIMS, preferred_element_type=F32)
        row = lax.broadcasted_iota(jnp.int32, (sub, width), 0) + r * sub
        col = lax.broadcasted_iota(jnp.int32, (sub, width), 1)
        _, l, acc = _online_softmax_step(carry[r], jnp.where(row >= col, s, NEG_INF), v)
        o_ref[r * sub:(r + 1) * sub, :] = (acc / l).astype(BF16)


def _fox_kernel(q_ref, k_ref, v_ref, cp_ref, o_ref, kaug_ref, *, t, sub):
    h = pl.program_id(1)
    qi = pl.program_id(2)

    @pl.when(qi == 0)
    def _():
        kaug_ref[:, :HEAD_DIM] = k_ref[...]
        lane = lax.broadcasted_iota(jnp.int32, cp_ref.shape, 1)
        mine = (lane >= 3 * h) & (lane < 3 * h + 3)
        kaug_ref[:, HEAD_DIM:] = jnp.where(mine, cp_ref[...], jnp.zeros_like(cp_ref))

    lane = lax.broadcasted_iota(jnp.int32, (t, LANES), 1)
    ones = jnp.where((lane >= 3 * h) & (lane < 3 * h + 3), 1.0, 0.0).astype(BF16)
    q_aug = jnp.concatenate([q_ref[...], ones], axis=1)
    _flash_attention(q_aug, kaug_ref, v_ref, o_ref, qi, t=t, sub=sub)


def _fox_attention(proj, cp, *, batch, seq, heads, k_col, v_col):
    n = proj.shape[0]
    t = min(ATTN_TILE, seq)
    sub = min(ATTN_SUB, t)
    nq = seq // t
    return pl.pallas_call(
        functools.partial(_fox_kernel, t=t, sub=sub),
        grid=(batch, heads, nq),
        in_specs=[
            pl.BlockSpec((t, HEAD_DIM), lambda b, h, qi: (b * nq + qi, h)),
            pl.BlockSpec((seq, HEAD_DIM), lambda b, h, qi: (b, k_col + h)),
            pl.BlockSpec((seq, HEAD_DIM), lambda b, h, qi: (b, v_col + h)),
            pl.BlockSpec((seq, LANES), lambda b, h, qi: (b, 0)),
        ],
        out_specs=pl.BlockSpec((t, HEAD_DIM), lambda b, h, qi: (b * nq + qi, h)),
        out_shape=jax.ShapeDtypeStruct((n, heads * HEAD_DIM), BF16),
        scratch_shapes=[pltpu.VMEM((seq, 2 * HEAD_DIM), BF16)],
        compiler_params=_params("arbitrary", "arbitrary", "arbitrary"),
        name="fox_attention",
    )(proj, proj, proj, cp)


def _moba_kernel(q_ref, k_ref, v_ref, kbar_ref, o_ref, kaug_ref, *, t, sub, nb):
    qi = pl.program_id(2)
    seq = k_ref.shape[0]

    @pl.when(qi == 0)
    def _():
        kaug_ref[:, :HEAD_DIM] = k_ref[...]
        r = lax.broadcasted_iota(jnp.int32, (seq, LANES), 0)
        c = lax.broadcasted_iota(jnp.int32, (seq, LANES), 1)
        kaug_ref[:, HEAD_DIM:] = jnp.where(r // MOBA_BLOCK == c, 1.0, 0.0).astype(BF16)

    q = q_ref[...]
    gate = jnp.zeros((nb, t), F32)
    for part in _split3(kbar_ref[0]):
        gate = gate + lax.dot_general(part, q, NT_DIMS, preferred_element_type=F32)
    blk = lax.broadcasted_iota(jnp.int32, (nb, t), 0)
    own = qi * (t // MOBA_BLOCK) + lax.broadcasted_iota(jnp.int32, (nb, t), 1) // MOBA_BLOCK
    valid = blk < own
    g = jnp.where(valid, gate, NEG_INF)
    rank = jnp.zeros((nb, t), jnp.int32)
    for m in range(nb):
        gm = g[m:m + 1, :]
        beats = (gm > g) | ((gm == g) & (blk > m))
        rank = rank + jnp.where(beats, 1, 0)
    allowed = (valid & (rank < MOBA_TOPK)) | (blk == own)
    bias_t = jnp.where(allowed, 0.0, NEG_INF)
    bias = jnp.concatenate([bias_t, jnp.zeros((LANES - nb, t), F32)], axis=0).T
    q_aug = jnp.concatenate([q, bias.astype(BF16)], axis=1)
    _flash_attention(q_aug, kaug_ref, v_ref, o_ref, qi, t=t, sub=sub)


def _moba_attention(proj, kbar, *, batch, seq, heads, q_col, k_col, v_col):
    n = proj.shape[0]
    nb = seq // MOBA_BLOCK
    assert nb <= LANES
    t = min(ATTN_TILE, seq)
    nq = seq // t
    return pl.pallas_call(
        functools.partial(_moba_kernel, t=t, sub=MOBA_BLOCK, nb=nb),
        grid=(batch, heads, nq),
        in_specs=[
            pl.BlockSpec((t, HEAD_DIM), lambda b, h, qi: (b * nq + qi, q_col + h)),
            pl.BlockSpec((seq, HEAD_DIM), lambda b, h, qi: (b, k_col + h)),
            pl.BlockSpec((seq, HEAD_DIM), lambda b, h, qi: (b, v_col + h)),
            pl.BlockSpec((1, nb, HEAD_DIM), lambda b, h, qi: (b, 0, h)),
        ],
        out_specs=pl.BlockSpec((t, HEAD_DIM), lambda b, h, qi: (b * nq + qi, h)),
        out_shape=jax.ShapeDtypeStruct((n, heads * HEAD_DIM), BF16),
        scratch_shapes=[pltpu.VMEM((seq, 2 * HEAD_DIM), BF16)],
        compiler_params=_params("arbitrary", "arbitrary", "arbitrary"),
        name="moba_attention",
    )(proj, proj, proj, kbar)


def _layer_norm_rows(z, g, b):
    mu = jnp.mean(z, axis=-1, keepdims=True)
    zc = z - mu
    var = jnp.mean(zc * zc, axis=-1, keepdims=True)
    return zc * lax.rsqrt(var + LN_EPS) * g + b


def _out_proj_kernel(of_ref, om_ref, w_ref, x_ref, g_ref, b_ref, o_ref, *, tn, fox_w, alpha):
    j = pl.program_id(1)
    acc = jnp.dot(of_ref[...], w_ref[0, :fox_w, :], preferred_element_type=F32)
    acc = acc + jnp.dot(om_ref[...], w_ref[0, fox_w:, :], preferred_element_type=F32)
    o_ref[:, pl.ds(pl.multiple_of(j * tn, tn), tn)] = alpha * x_ref[...] + acc

    @pl.when(j == pl.num_programs(1) - 1)
    def _():
        o_ref[...] = _layer_norm_rows(o_ref[...], g_ref[...], b_ref[...])


def _out_proj_ln(o_f, o_m, w, layer, x, g, b, *, alpha):
    n, d = x.shape
    fox_w, moba_w = o_f.shape[1], o_m.shape[1]
    tm = min(512, n)
    tn = min(512, d)
    kern = functools.partial(_out_proj_kernel, tn=tn, fox_w=fox_w, alpha=alpha)
    return pl.pallas_call(
        kern,
        grid=(n // tm, d // tn),
        in_specs=[
            pl.BlockSpec((tm, fox_w), lambda i, j: (i, 0)),
            pl.BlockSpec((tm, moba_w), lambda i, j: (i, 0)),
            pl.BlockSpec((1, fox_w + moba_w, tn), lambda i, j: (layer, 0, j)),
            pl.BlockSpec((tm, tn), lambda i, j: (i, j)),
            pl.BlockSpec((1, d), lambda i, j: (0, 0)),
            pl.BlockSpec((1, d), lambda i, j: (0, 0)),
        ],
        out_specs=pl.BlockSpec((tm, d), lambda i, j: (i, 0)),
        out_shape=jax.ShapeDtypeStruct((n, d), F32),
        compiler_params=_params("arbitrary", "arbitrary"),
        name="out_proj_ln",
    )(o_f, o_m, w, x, g, b)


def _router_kernel(x_ref, wt_ref, bias_ref, idx_ref, wts_ref, *, n_exp):
    logits = lax.dot_general(wt_ref[...], x_ref[...], NT_DIMS, precision=lax.Precision.HIGHEST,
                             preferred_element_type=F32)
    ex = jnp.exp(logits - logits.max(axis=0, keepdims=True))
    probs = ex / ex.sum(axis=0, keepdims=True)
    sel = probs + bias_ref[...]
    per = n_exp // N_GROUPS
    gscore = []
    for gidx in range(N_GROUPS):
        r = [sel[gidx * per + a:gidx * per + a + 1, :] for a in range(per)]
        best = None
        for a in range(per):
            for b in range(a + 1, per):
                pair = r[a] + r[b]
                best = pair if best is None else jnp.maximum(best, pair)
        gscore.append(best)
    gmax = functools.reduce(jnp.maximum, gscore)
    g_idx = jnp.full_like(gmax, N_GROUPS - 1).astype(jnp.int32)
    for gidx in range(N_GROUPS - 2, -1, -1):
        g_idx = jnp.where(gscore[gidx] == gmax, gidx, g_idx)
    erow = lax.broadcasted_iota(jnp.int32, sel.shape, 0)
    masked = jnp.where(erow // per == g_idx, sel, NEG_INF)
    v1 = masked.max(axis=0, keepdims=True)
    i1 = jnp.min(jnp.where(masked == v1, erow, n_exp), axis=0, keepdims=True)
    masked2 = jnp.where(erow == i1, -jnp.inf, masked)
    v2 = masked2.max(axis=0, keepdims=True)
    i2 = jnp.min(jnp.where(masked2 == v2, erow, n_exp), axis=0, keepdims=True)
    w1 = jnp.sum(jnp.where(erow == i1, probs, 0.0), axis=0, keepdims=True)
    w2 = jnp.sum(jnp.where(erow == i2, probs, 0.0), axis=0, keepdims=True)
    tot = w1 + w2
    idx_ref[...] = jnp.concatenate([i1, i2], axis=0)
    wts_ref[...] = jnp.concatenate([w1 / tot, w2 / tot], axis=0)


def _router(x, w_router_t, bias_col):
    n, d = x.shape
    n_exp = w_router_t.shape[0]
    tm = min(512, n)
    return pl.pallas_call(
        functools.partial(_router_kernel, n_exp=n_exp),
        grid=(n // tm,),
        in_specs=[
            pl.BlockSpec((tm, d), lambda i: (i, 0)),
            pl.BlockSpec((n_exp, d), lambda i: (0, 0)),
            pl.BlockSpec((n_exp, 1), lambda i: (0, 0)),
        ],
        out_specs=[
            pl.BlockSpec((TOP_K, tm), lambda i: (0, i)),
            pl.BlockSpec((TOP_K, tm), lambda i: (0, i)),
        ],
        out_shape=[
            jax.ShapeDtypeStruct((TOP_K, n), jnp.int32),
            jax.ShapeDtypeStruct((TOP_K, n), F32),
        ],
        compiler_params=_params("arbitrary"),
        name="router",
    )(x, w_router_t, bias_col)


def _dispatch(idx, n_exp):
    k, n = idx.shape
    e = idx.reshape(-1)
    onehot = (e[:, None] == jnp.arange(n_exp, dtype=jnp.int32)[None, :]).astype(jnp.int32)
    cums = jnp.cumsum(onehot, axis=0)
    rank = jnp.take_along_axis(cums, e[:, None], axis=1)[:, 0] - 1
    counts = cums[-1]
    padded = (counts + MOE_ROWS - 1) // MOE_ROWS * MOE_ROWS
    pend = jnp.cumsum(padded)
    dest = (pend - padded)[e] + rank
    n_blocks = (k * n) // MOE_ROWS + n_exp
    tok = jnp.tile(jnp.arange(n, dtype=jnp.int32), k)
    row_tok = jnp.zeros((n_blocks * MOE_ROWS,), jnp.int32).at[dest].set(tok)
    blk_start = jnp.arange(n_blocks, dtype=jnp.int32) * MOE_ROWS
    blk_expert = jnp.minimum(
        jnp.sum((pend[None, :] <= blk_start[:, None]).astype(jnp.int32), axis=1), n_exp - 1)
    n_used = (pend[-1:] // MOE_ROWS).astype(jnp.int32)
    return dest.astype(jnp.int32), row_tok, blk_expert, n_used


def _start_row_gather(idx_ref, base, count, src_hbm, dst_ref, sem):
    for i in range(count):
        r = idx_ref[base + i]
        pltpu.make_async_copy(src_hbm.at[pl.ds(r, 1), :], dst_ref.at[pl.ds(i, 1), :], sem).start()


def _wait_row_gather(count, src_hbm, dst_ref, sem):
    pltpu.make_async_copy(src_hbm.at[pl.ds(0, count), :], dst_ref, sem).wait()


def _moe_up_kernel(blk_ref, tok_ref, x_hbm, wg_ref, wu_ref, h_ref, xg0_ref, xg1_ref, sems):
    b = pl.program_id(0)
    last = pl.num_programs(0) - 1
    nxt_base = jnp.minimum(b + 1, last) * MOE_ROWS

    @pl.when(b == 0)
    def _():
        _start_row_gather(tok_ref, 0, MOE_ROWS, x_hbm, xg0_ref, sems.at[0])

    def step(cur_ref, cur_sem, nxt_ref, nxt_sem):
        _wait_row_gather(MOE_ROWS, x_hbm, cur_ref, cur_sem)
        _start_row_gather(tok_ref, nxt_base, MOE_ROWS, x_hbm, nxt_ref, nxt_sem)
        xb = cur_ref[...].astype(BF16)
        g = jnp.dot(xb, wg_ref[0, 0], preferred_element_type=F32)
        u = jnp.dot(xb, wu_ref[0, 0], preferred_element_type=F32)
        h_ref[...] = (g * jax.nn.sigmoid(g) * u).astype(BF16)

    @pl.when(b % 2 == 0)
    def _():
        step(xg0_ref, sems.at[0], xg1_ref, sems.at[1])

    @pl.when(b % 2 == 1)
    def _():
        step(xg1_ref, sems.at[1], xg0_ref, sems.at[0])

    @pl.when(b == last)
    def _():
        @pl.when(b % 2 == 0)
        def _():
            _wait_row_gather(MOE_ROWS, x_hbm, xg1_ref, sems.at[1])

        @pl.when(b % 2 == 1)
        def _():
            _wait_row_gather(MOE_ROWS, x_hbm, xg0_ref, sems.at[0])


def _moe_up(x, w_gate, w_up, layer, blk_expert, row_tok):
    n, d = x.shape
    n_blocks = blk_expert.shape[0]
    d_ff = w_gate.shape[-1]
    return pl.pallas_call(
        _moe_up_kernel,
        grid_spec=pltpu.PrefetchScalarGridSpec(
            num_scalar_prefetch=2,
            grid=(n_blocks,),
            in_specs=[
                pl.BlockSpec(memory_space=pl.ANY),
                pl.BlockSpec((1, 1, d, d_ff), lambda b, blk, tok: (layer, blk[b], 0, 0)),
                pl.BlockSpec((1, 1, d, d_ff), lambda b, blk, tok: (layer, blk[b], 0, 0)),
            ],
            out_specs=pl.BlockSpec((MOE_ROWS, d_ff), lambda b, blk, tok: (b, 0)),
            scratch_shapes=[pltpu.VMEM((MOE_ROWS, d), F32), pltpu.VMEM((MOE_ROWS, d), F32),
                            pltpu.SemaphoreType.DMA((2,))],
        ),
        out_shape=jax.ShapeDtypeStruct((n_blocks * MOE_ROWS, d_ff), BF16),
        compiler_params=_params("arbitrary"),
        name="moe_up",
    )(blk_expert, row_tok, x, w_gate, w_up)


def _moe_down_kernel(blk_ref, nused_ref, h_ref, wd_ref, y_ref):
    b = pl.program_id(0)

    @pl.when(b < nused_ref[0])
    def _():
        y_ref[...] = jnp.dot(h_ref[...], wd_ref[0, 0], preferred_element_type=F32)

    @pl.when(b >= nused_ref[0])
    def _():
        y_ref[...] = jnp.zeros_like(y_ref)


def _moe_down(h, w_down, layer, blk_expert, n_used):
    n_blocks = blk_expert.shape[0]
    d_ff, d = w_down.shape[2:]
    return pl.pallas_call(
        _moe_down_kernel,
        grid_spec=pltpu.PrefetchScalarGridSpec(
            num_scalar_prefetch=2,
            grid=(n_blocks,),
            in_specs=[
                pl.BlockSpec((MOE_ROWS, d_ff), lambda b, blk, nu: (b, 0)),
                pl.BlockSpec((1, 1, d_ff, d), lambda b, blk, nu: (layer, blk[b], 0, 0)),
            ],
            out_specs=pl.BlockSpec((MOE_ROWS, d), lambda b, blk, nu: (b, 0)),
        ),
        out_shape=jax.ShapeDtypeStruct((n_blocks * MOE_ROWS, d), F32),
        compiler_params=_params("arbitrary"),
        name="moe_down",
    )(blk_expert, n_used, h, w_down)


def _combine_kernel(dest_ref, y_hbm, x_ref, wt_ref, g_ref, b_ref, o_ref, ob_ref,
                    ya0_ref, yb0_ref, ya1_ref, yb1_ref, sems, *, tc, n_tok, alpha):
    i = pl.program_id(0)
    last = pl.num_programs(0) - 1
    nxt_base = jnp.minimum(i + 1, last) * tc

    def start(base, ya_ref, yb_ref, slot):
        _start_row_gather(dest_ref, base, tc, y_hbm, ya_ref, sems.at[slot, 0])
        _start_row_gather(dest_ref, n_tok + base, tc, y_hbm, yb_ref, sems.at[slot, 1])

    def wait(ya_ref, yb_ref, slot):
        _wait_row_gather(tc, y_hbm, ya_ref, sems.at[slot, 0])
        _wait_row_gather(tc, y_hbm, yb_ref, sems.at[slot, 1])

    @pl.when(i == 0)
    def _():
        start(0, ya0_ref, yb0_ref, 0)

    def step(cur, nxt):
        wait(*cur)
        start(nxt_base, *nxt)
        moe = wt_ref[:, 0:1] * cur[0][...] + wt_ref[:, 1:2] * cur[1][...]
        out = _layer_norm_rows(alpha * x_ref[...] + moe, g_ref[...], b_ref[...])
        o_ref[...] = out
        ob_ref[...] = out.astype(BF16)

    slot0 = (ya0_ref, yb0_ref, 0)
    slot1 = (ya1_ref, yb1_ref, 1)

    @pl.when(i % 2 == 0)
    def _():
        step(slot0, slot1)

    @pl.when(i % 2 == 1)
    def _():
        step(slot1, slot0)

    @pl.when(i == last)
    def _():
        @pl.when(i % 2 == 0)
        def _():
            wait(*slot1)

        @pl.when(i % 2 == 1)
        def _():
            wait(*slot0)


def _combine_ln(y, x, wts_tok, dest, g, b, *, alpha):
    n, d = x.shape
    tc = min(256, n)
    kern = functools.partial(_combine_kernel, tc=tc, n_tok=n, alpha=alpha)
    return pl.pallas_call(
        kern,
        grid_spec=pltpu.PrefetchScalarGridSpec(
            num_scalar_prefetch=1,
            grid=(n // tc,),
            in_specs=[
                pl.BlockSpec(memory_space=pl.ANY),
                pl.BlockSpec((tc, d), lambda i, dest: (i, 0)),
                pl.BlockSpec((tc, TOP_K), lambda i, dest: (i, 0)),
                pl.BlockSpec((1, d), lambda i, dest: (0, 0)),
                pl.BlockSpec((1, d), lambda i, dest: (0, 0)),
            ],
            out_specs=[
                pl.BlockSpec((tc, d), lambda i, dest: (i, 0)),
                pl.BlockSpec((tc, d), lambda i, dest: (i, 0)),
            ],
            scratch_shapes=[pltpu.VMEM((tc, d), F32)] * 4 + [pltpu.SemaphoreType.DMA((2, 2))],
        ),
        out_shape=[jax.ShapeDtypeStruct((n, d), F32), jax.ShapeDtypeStruct((n, d), BF16)],
        compiler_params=_params("arbitrary"),
        name="combine_ln",
    )(dest, y, x, wts_tok, g, b)


def _rope_tables(seq):
    half = HEAD_DIM // 2
    inv = ROPE_THETA ** (-jnp.arange(half, dtype=F32) / half)
    ang = jnp.arange(seq, dtype=jnp.int32).astype(F32)[:, None] * inv[None, :]
    cos, sin = jnp.cos(ang), jnp.sin(ang)
    return jnp.concatenate([cos, cos], axis=-1), jnp.concatenate([-sin, sin], axis=-1)


def kernel(x, w_in, b_forget, w_out, ln1_g, ln1_b, w_router, router_bias, w_gate, w_up, w_down, ln2_g, ln2_b):
    batch, seq, d = x.shape
    depth = w_in.shape[0]
    fox_heads = b_forget.shape[-1]
    fox_w = fox_heads * HEAD_DIM
    moba_w = w_out.shape[1] - fox_w
    moba_heads = moba_w // HEAD_DIM
    n_exp = w_router.shape[1]
    n = batch * seq
    alpha = (2.0 * depth) ** 0.25
    qkv_w = 3 * (fox_w + moba_w)
    hb = HEAD_DIM

    cos, sin = _rope_tables(seq)
    w_router_t = w_router.T
    bias_col = router_bias.reshape(n_exp, 1)
    xf = x.reshape(n, d)
    xb = xf.astype(BF16)
    w_in_t = jnp.swapaxes(w_in, 1, 2)
    w_out_b = w_out.astype(BF16)
    w_gate_b, w_up_b, w_down_b = w_gate.astype(BF16), w_up.astype(BF16), w_down.astype(BF16)

    for l in range(depth):
        w_f = jnp.pad(w_in_t[l, qkv_w:, :], ((0, LANES - fox_heads), (0, 0)))
        b_f = jnp.pad(b_forget[l], (0, LANES - fox_heads)).reshape(1, LANES)

        proj, kbar = _in_proj(xb, w_in_t, l, cos, sin, fox_w=fox_w, moba_w=moba_w, seq=seq)
        cp = _forget_cumsum(xb, w_f, b_f, batch=batch, seq=seq, heads=fox_heads)
        o_f = _fox_attention(proj, cp, batch=batch, seq=seq, heads=fox_heads,
                             k_col=fox_w // hb, v_col=2 * fox_w // hb)
        o_m = _moba_attention(proj, kbar.reshape(batch, seq // MOBA_BLOCK, moba_w),
                              batch=batch, seq=seq, heads=moba_heads,
                              q_col=3 * fox_w // hb, k_col=(3 * fox_w + moba_w) // hb,
                              v_col=(3 * fox_w + 2 * moba_w) // hb)
        x1 = _out_proj_ln(o_f, o_m, w_out_b, l, xf,
                          ln1_g[l].reshape(1, d), ln1_b[l].reshape(1, d), alpha=alpha)

        idx, wts = _router(x1, w_router_t, bias_col)
        dest, row_tok, blk_expert, n_used = _dispatch(idx, n_exp)
        h = _moe_up(x1, w_gate_b, w_up_b, l, blk_expert, row_tok)
        y = _moe_down(h, w_down_b, l, blk_expert, n_used)
        xf, xb = _combine_ln(y, x1, wts.T, dest, ln2_g[l].reshape(1, d), ln2_b[l].reshape(1, d),
                             alpha=alpha)

    return xf.reshape(batch, seq, d)
```

```python
import functools
import math

import jax
import jax.numpy as jnp
from jax import lax
from jax.experimental import pallas as pl
from jax.experimental.pallas import tpu as pltpu

HEAD_DIM = 128
MOBA_BLOCK = 256
MOBA_TOPK = 3
ROPE_THETA = 10000.0
N_GROUPS = 4
TOP_K = 2
LN_EPS = 1e-5
NEG_INF = -1e30
LOG2E = math.log2(math.e)
Q_SCALE = HEAD_DIM ** -0.5 * LOG2E
MOE_ROWS = 256
LANES = 128
VMEM_LIMIT = 56 * 1024 * 1024
ATTN_TILE = 1024
ATTN_SUB = 256

F32 = jnp.float32
BF16 = jnp.bfloat16
NT_DIMS = (((1,), (1,)), ((), ()))


def _params(*sem):
    return pltpu.CompilerParams(dimension_semantics=sem, vmem_limit_bytes=VMEM_LIMIT)


def _split3(a):
    hi = a.astype(BF16)
    r1 = a - hi.astype(F32)
    mid = r1.astype(BF16)
    lo = (r1 - mid.astype(F32)).astype(BF16)
    return hi, mid, lo


def _in_proj_kernel(x_ref, w_ref, cos_ref, sin_ref, o_ref, kbar_ref, *, tm, tn, j0, j1, j2, j3):
    acc = lax.dot_general(x_ref[...], w_ref[0].astype(BF16), NT_DIMS, preferred_element_type=F32)
    j = pl.program_id(1)

    @pl.when(j < j0)
    def _():
        o_ref[...] = (acc * Q_SCALE).astype(BF16)

    @pl.when(((j >= j0) & (j < j1)) | (j >= j3))
    def _():
        o_ref[...] = acc.astype(BF16)

    def roped(c):
        ch = acc[:, c * HEAD_DIM:(c + 1) * HEAD_DIM]
        return ch * cos_ref[...] + pltpu.roll(ch, HEAD_DIM // 2, 1) * sin_ref[...]

    @pl.when((j >= j1) & (j < j2))
    def _():
        for c in range(tn // HEAD_DIM):
            o_ref[:, c * HEAD_DIM:(c + 1) * HEAD_DIM] = (roped(c) * Q_SCALE).astype(BF16)

    @pl.when((j >= j2) & (j < j3))
    def _():
        for c in range(tn // HEAD_DIM):
            r = roped(c)
            o_ref[:, c * HEAD_DIM:(c + 1) * HEAD_DIM] = r.astype(BF16)
            kbar_ref[0, :, c * HEAD_DIM:(c + 1) * HEAD_DIM] = (
                r.reshape(tm // MOBA_BLOCK, MOBA_BLOCK, HEAD_DIM).sum(axis=1) * (1.0 / MOBA_BLOCK))


def _in_proj(xb, w_in_t, layer, cos, sin, *, fox_w, moba_w, seq):
    n, d = xb.shape
    tm = min(1024, seq)
    tn = min(512, fox_w, moba_w)
    fq, mq = fox_w // tn, moba_w // tn
    j0, j1 = fq, 3 * fq
    j2, j3 = j1 + mq, j1 + 2 * mq
    nj = j1 + 3 * mq
    spb = seq // tm
    kern = functools.partial(_in_proj_kernel, tm=tm, tn=tn, j0=j0, j1=j1, j2=j2, j3=j3)
    return pl.pallas_call(
        kern,
        grid=(n // tm, nj),
        in_specs=[
            pl.BlockSpec((tm, d), lambda i, j: (i, 0)),
            pl.BlockSpec((1, tn, d), lambda i, j: (layer, j, 0)),
            pl.BlockSpec((tm, HEAD_DIM), lambda i, j: (i % spb, 0)),
            pl.BlockSpec((tm, HEAD_DIM), lambda i, j: (i % spb, 0)),
        ],
        out_specs=[
            pl.BlockSpec((tm, tn), lambda i, j: (i, j)),
            pl.BlockSpec((1, tm // MOBA_BLOCK, tn),
                         lambda i, j: (i, 0, jnp.clip(j - j2, 0, mq - 1))),
        ],
        out_shape=[
            jax.ShapeDtypeStruct((n, 3 * (fox_w + moba_w)), BF16),
            jax.ShapeDtypeStruct((n // tm, tm // MOBA_BLOCK, moba_w), F32),
        ],
        compiler_params=_params("arbitrary", "arbitrary"),
        name="in_proj",
    )(xb, w_in_t, cos, sin)


def _forget_kernel(x_ref, w_ref, b_ref, cp_ref, carry_ref, *, ts, heads):
    @pl.when(pl.program_id(1) == 0)
    def _():
        carry_ref[...] = jnp.zeros_like(carry_ref)

    f = lax.dot_general(x_ref[...], w_ref[...].astype(BF16), NT_DIMS, preferred_element_type=F32) + b_ref[...]
    lf = jnp.minimum(f, 0.0) - jnp.log1p(jnp.exp(-jnp.abs(f)))
    row = lax.broadcasted_iota(jnp.int32, (ts, ts), 0)
    col = lax.broadcasted_iota(jnp.int32, (ts, ts), 1)
    tri = jnp.where(row >= col, 1.0, 0.0).astype(BF16)
    cs = carry_ref[...]
    for part in _split3(lf):
        cs = cs + jnp.dot(tri, part, preferred_element_type=F32)
    carry_ref[...] = cs[ts - 1:ts, :]

    lane = lax.broadcasted_iota(jnp.int32, (ts, LANES), 1)
    neg = jnp.where(lane < heads, -LOG2E * cs, 0.0)
    erow = lax.broadcasted_iota(jnp.int32, (LANES, LANES), 0)
    ecol = lax.broadcasted_iota(jnp.int32, (LANES, LANES), 1)
    out = jnp.zeros((ts, LANES), F32)
    for j, part in enumerate(_split3(neg)):
        spread = jnp.where(ecol == 3 * erow + j, 1.0, 0.0).astype(BF16)
        out = out + jnp.dot(part, spread, preferred_element_type=F32)
    cp_ref[...] = out.astype(BF16)


def _forget_cumsum(xb, w_f, b_f, *, batch, seq, heads):
    n, d = xb.shape
    assert 3 * heads <= LANES
    ts = min(512, seq)
    spb = seq // ts
    return pl.pallas_call(
        functools.partial(_forget_kernel, ts=ts, heads=heads),
        grid=(batch, spb),
        in_specs=[
            pl.BlockSpec((ts, d), lambda b, s: (b * spb + s, 0)),
            pl.BlockSpec((LANES, d), lambda b, s: (0, 0)),
            pl.BlockSpec((1, LANES), lambda b, s: (0, 0)),
        ],
        out_specs=pl.BlockSpec((ts, LANES), lambda b, s: (b * spb + s, 0)),
        out_shape=jax.ShapeDtypeStruct((n, LANES), BF16),
        scratch_shapes=[pltpu.VMEM((1, LANES), F32)],
        compiler_params=_params("arbitrary", "arbitrary"),
        name="forget_cumsum",
    )(xb, w_f, b_f)


def _softmax_update(m, l, s_t):
    m_new = jnp.maximum(m, s_t.max(axis=0, keepdims=True))
    alpha = jnp.exp2(m - m_new)
    p_t = jnp.exp2(s_t - m_new)
    return m_new, alpha * l + p_t.sum(axis=0, keepdims=True), alpha, p_t.astype(BF16)


def _attend_tile(q_subs, keys, values_t, masks, carry):
    n_sub = len(q_subs)

    def scores(r):
        s_t = jnp.dot(keys[r], q_subs[r], preferred_element_type=F32)
        return s_t if masks[r] is None else jnp.where(masks[r], s_t, NEG_INF)

    s = {r: scores(r) for r in range(min(2, n_sub))}
    out = []
    for r in range(n_sub):
        m, l, acc_t = carry[r]
        m, l, alpha, p_t = _softmax_update(m, l, s.pop(r))
        if r + 2 < n_sub:
            s[r + 2] = scores(r + 2)
        acc_t = alpha * acc_t + jnp.dot(values_t[r], p_t, preferred_element_type=F32)
        out.append((m, l, acc_t))
    return tuple(out)


def _transpose_bf16(a):
    return a.astype(F32).T.astype(BF16)


def _store_keys_values(k_ref, v_ref, kaug_ref, vt_ref, aug, *, chunk):
    seq = k_ref.shape[0]
    kaug_ref[:, :HEAD_DIM] = k_ref[...]
    kaug_ref[:, HEAD_DIM:] = aug
    for c in range(seq // chunk):
        vt_ref[:, c * chunk:(c + 1) * chunk] = _transpose_bf16(v_ref[c * chunk:(c + 1) * chunk, :])


def _flash_attention(qt_aug, kaug_ref, vt_ref, o_ref, qi, *, t, sub):
    n_sub = t // sub
    q_subs = [qt_aug[:, r * sub:(r + 1) * sub] for r in range(n_sub)]

    def past(ki, carry):
        start = pl.multiple_of(ki * t, t)
        k = kaug_ref[pl.ds(start, t), :]
        v_t = vt_ref[:, pl.ds(start, t)]
        return _attend_tile(q_subs, [k] * n_sub, [v_t] * n_sub, [None] * n_sub, carry)

    init = tuple((jnp.full((1, sub), -jnp.inf, F32), jnp.zeros((1, sub), F32), jnp.zeros((HEAD_DIM, sub), F32))
                 for _ in range(n_sub))
    carry = lax.fori_loop(0, qi, past, init)

    start = pl.multiple_of(qi * t, t)
    keys, values_t, masks = [], [], []
    for r in range(n_sub):
        width = (r + 1) * sub
        keys.append(kaug_ref[pl.ds(start, width), :])
        values_t.append(vt_ref[:, pl.ds(start, width)])
        key = lax.broadcasted_iota(jnp.int32, (width, sub), 0)
        qry = lax.broadcasted_iota(jnp.int32, (width, sub), 1) + r * sub
        masks.append(key <= qry)
    carry = _attend_tile(q_subs, keys, values_t, masks, carry)
    for r in range(n_sub):
        _, l, acc_t = carry[r]
        o_ref[r * sub:(r + 1) * sub, :] = (acc_t / l).T.astype(BF16)


def _fox_kernel(q_ref, k_ref, v_ref, cp_ref, o_ref, kaug_ref, vt_ref, *, t, sub):
    h = pl.program_id(1)
    qi = pl.program_id(2)

    @pl.when(qi == 0)
    def _():
        lane = lax.broadcasted_iota(jnp.int32, cp_ref.shape, 1)
        mine = (lane >= 3 * h) & (lane < 3 * h + 3)
        _store_keys_values(k_ref, v_ref, kaug_ref, vt_ref,
                           jnp.where(mine, cp_ref[...], jnp.zeros_like(cp_ref)), chunk=t)

    row = lax.broadcasted_iota(jnp.int32, (LANES, t), 0)
    ones_t = jnp.where((row >= 3 * h) & (row < 3 * h + 3), 1.0, 0.0).astype(BF16)
    qt_aug = jnp.concatenate([_transpose_bf16(q_ref[...]), ones_t], axis=0)
    _flash_attention(qt_aug, kaug_ref, vt_ref, o_ref, qi, t=t, sub=sub)


def _fox_attention(proj, cp, *, batch, seq, heads, k_col, v_col):
    n = proj.shape[0]
    t = min(ATTN_TILE, seq)
    sub = min(ATTN_SUB, t)
    nq = seq // t
    return pl.pallas_call(
        functools.partial(_fox_kernel, t=t, sub=sub),
        grid=(batch, heads, nq),
        in_specs=[
            pl.BlockSpec((t, HEAD_DIM), lambda b, h, qi: (b * nq + qi, h)),
            pl.BlockSpec((seq, HEAD_DIM), lambda b, h, qi: (b, k_col + h)),
            pl.BlockSpec((seq, HEAD_DIM), lambda b, h, qi: (b, v_col + h)),
            pl.BlockSpec((seq, LANES), lambda b, h, qi: (b, 0)),
        ],
        out_specs=pl.BlockSpec((t, HEAD_DIM), lambda b, h, qi: (b * nq + qi, h)),
        out_shape=jax.ShapeDtypeStruct((n, heads * HEAD_DIM), BF16),
        scratch_shapes=[pltpu.VMEM((seq, 2 * HEAD_DIM), BF16), pltpu.VMEM((HEAD_DIM, seq), BF16)],
        compiler_params=_params("arbitrary", "arbitrary", "arbitrary"),
        name="fox_attention",
    )(proj, proj, proj, cp)


def _moba_kernel(q_ref, k_ref, v_ref, kbar_ref, o_ref, kaug_ref, vt_ref, *, t, sub, nb):
    qi = pl.program_id(2)
    seq = k_ref.shape[0]

    @pl.when(qi == 0)
    def _():
        r = lax.broadcasted_iota(jnp.int32, (seq, LANES), 0)
        c = lax.broadcasted_iota(jnp.int32, (seq, LANES), 1)
        _store_keys_values(k_ref, v_ref, kaug_ref, vt_ref,
                           jnp.where(r // MOBA_BLOCK == c, 1.0, 0.0).astype(BF16), chunk=t)

    q = q_ref[...]
    gate = jnp.zeros((nb, t), F32)
    for part in _split3(kbar_ref[0]):
        gate = gate + lax.dot_general(part, q, NT_DIMS, preferred_element_type=F32)
    blk = lax.broadcasted_iota(jnp.int32, (nb, t), 0)
    own = qi * (t // MOBA_BLOCK) + lax.broadcasted_iota(jnp.int32, (nb, t), 1) // MOBA_BLOCK
    valid = blk < own
    g = jnp.where(valid, gate, NEG_INF)
    rank = jnp.zeros((nb, t), jnp.int32)
    for m in range(nb):
        gm = g[m:m + 1, :]
        beats = (gm > g) | ((gm == g) & (blk > m))
        rank = rank + jnp.where(beats, 1, 0)
    allowed = (valid & (rank < MOBA_TOPK)) | (blk == own)
    bias_t = jnp.where(allowed, 0.0, NEG_INF).astype(BF16)
    qt_aug = jnp.concatenate([_transpose_bf16(q), bias_t, jnp.zeros((LANES - nb, t), BF16)], axis=0)
    _flash_attention(qt_aug, kaug_ref, vt_ref, o_ref, qi, t=t, sub=sub)


def _moba_attention(proj, kbar, *, batch, seq, heads, q_col, k_col, v_col):
    n = proj.shape[0]
    nb = seq // MOBA_BLOCK
    assert nb <= LANES
    t = min(ATTN_TILE, seq)
    nq = seq // t
    return pl.pallas_call(
        functools.partial(_moba_kernel, t=t, sub=MOBA_BLOCK, nb=nb),
        grid=(batch, heads, nq),
        in_specs=[
            pl.BlockSpec((t, HEAD_DIM), lambda b, h, qi: (b * nq + qi, q_col + h)),
            pl.BlockSpec((seq, HEAD_DIM), lambda b, h, qi: (b, k_col + h)),
            pl.BlockSpec((seq, HEAD_DIM), lambda b, h, qi: (b, v_col + h)),
            pl.BlockSpec((1, nb, HEAD_DIM), lambda b, h, qi: (b, 0, h)),
        ],
        out_specs=pl.BlockSpec((t, HEAD_DIM), lambda b, h, qi: (b * nq + qi, h)),
        out_shape=jax.ShapeDtypeStruct((n, heads * HEAD_DIM), BF16),
        scratch_shapes=[pltpu.VMEM((seq, 2 * HEAD_DIM), BF16), pltpu.VMEM((HEAD_DIM, seq), BF16)],
        compiler_params=_params("arbitrary", "arbitrary", "arbitrary"),
        name="moba_attention",
    )(proj, proj, proj, kbar)


def _layer_norm_rows(z, g, b):
    mu = jnp.mean(z, axis=-1, keepdims=True)
    zc = z - mu
    var = jnp.mean(zc * zc, axis=-1, keepdims=True)
    return zc * lax.rsqrt(var + LN_EPS) * g + b


def _pack_bf16_pairs(x):
    half = x.shape[1] // 2
    lo = pltpu.bitcast(x[:, :half].astype(BF16).astype(F32), jnp.uint32)
    hi = pltpu.bitcast(x[:, half:].astype(BF16).astype(F32), jnp.uint32)
    return (lo >> 16) | (hi & jnp.uint32(0xFFFF0000))


def _unpack_bf16_pairs(p):
    lo = pltpu.bitcast(p << 16, F32).astype(BF16)
    hi = pltpu.bitcast(p & jnp.uint32(0xFFFF0000), F32).astype(BF16)
    return lo, hi


def _out_proj_kernel(of_ref, om_ref, w_ref, x_ref, g_ref, b_ref, o_ref, op_ref, *, tn, fox_w, alpha):
    j = pl.program_id(1)
    acc = jnp.dot(of_ref[...], w_ref[0, :fox_w, :], preferred_element_type=F32)
    acc = acc + jnp.dot(om_ref[...], w_ref[0, fox_w:, :], preferred_element_type=F32)
    o_ref[:, pl.ds(pl.multiple_of(j * tn, tn), tn)] = alpha * x_ref[...] + acc

    @pl.when(j == pl.num_programs(1) - 1)
    def _():
        out = _layer_norm_rows(o_ref[...], g_ref[...], b_ref[...])
        o_ref[...] = out
        op_ref[...] = _pack_bf16_pairs(out)


def _out_proj_ln(o_f, o_m, w, layer, x, g, b, *, alpha):
    n, d = x.shape
    fox_w, moba_w = o_f.shape[1], o_m.shape[1]
    tm = min(512, n)
    tn = min(512, d)
    kern = functools.partial(_out_proj_kernel, tn=tn, fox_w=fox_w, alpha=alpha)
    return pl.pallas_call(
        kern,
        grid=(n // tm, d // tn),
        in_specs=[
            pl.BlockSpec((tm, fox_w), lambda i, j: (i, 0)),
            pl.BlockSpec((tm, moba_w), lambda i, j: (i, 0)),
            pl.BlockSpec((1, fox_w + moba_w, tn), lambda i, j: (layer, 0, j)),
            pl.BlockSpec((tm, tn), lambda i, j: (i, j)),
            pl.BlockSpec((1, d), lambda i, j: (0, 0)),
            pl.BlockSpec((1, d), lambda i, j: (0, 0)),
        ],
        out_specs=[pl.BlockSpec((tm, d), lambda i, j: (i, 0)),
                   pl.BlockSpec((tm, d // 2), lambda i, j: (i, 0))],
        out_shape=[jax.ShapeDtypeStruct((n, d), F32), jax.ShapeDtypeStruct((n, d // 2), jnp.uint32)],
        compiler_params=_params("arbitrary", "arbitrary"),
        name="out_proj_ln",
    )(o_f, o_m, w, x, g, b)


def _router_kernel(x_ref, wt_ref, bias_ref, idx_ref, wts_ref, *, n_exp):
    logits = lax.dot_general(wt_ref[...], x_ref[...], NT_DIMS, precision=lax.Precision.HIGHEST,
                             preferred_element_type=F32)
    ex = jnp.exp(logits - logits.max(axis=0, keepdims=True))
    probs = ex / ex.sum(axis=0, keepdims=True)
    sel = probs + bias_ref[...]
    per = n_exp // N_GROUPS
    gscore = []
    for gidx in range(N_GROUPS):
        r = [sel[gidx * per + a:gidx * per + a + 1, :] for a in range(per)]
        best = None
        for a in range(per):
            for b in range(a + 1, per):
                pair = r[a] + r[b]
                best = pair if best is None else jnp.maximum(best, pair)
        gscore.append(best)
    gmax = functools.reduce(jnp.maximum, gscore)
    g_idx = jnp.full_like(gmax, N_GROUPS - 1).astype(jnp.int32)
    for gidx in range(N_GROUPS - 2, -1, -1):
        g_idx = jnp.where(gscore[gidx] == gmax, gidx, g_idx)
    erow = lax.broadcasted_iota(jnp.int32, sel.shape, 0)
    masked = jnp.where(erow // per == g_idx, sel, NEG_INF)
    v1 = masked.max(axis=0, keepdims=True)
    i1 = jnp.min(jnp.where(masked == v1, erow, n_exp), axis=0, keepdims=True)
    masked2 = jnp.where(erow == i1, -jnp.inf, masked)
    v2 = masked2.max(axis=0, keepdims=True)
    i2 = jnp.min(jnp.where(masked2 == v2, erow, n_exp), axis=0, keepdims=True)
    w1 = jnp.sum(jnp.where(erow == i1, probs, 0.0), axis=0, keepdims=True)
    w2 = jnp.sum(jnp.where(erow == i2, probs, 0.0), axis=0, keepdims=True)
    tot = w1 + w2
    idx_ref[...] = jnp.concatenate([i1, i2], axis=0)
    wts_ref[...] = jnp.concatenate([w1 / tot, w2 / tot], axis=0)


def _router(x, w_router_t, bias_col):
    n, d = x.shape
    n_exp = w_router_t.shape[0]
    tm = min(512, n)
    return pl.pallas_call(
        functools.partial(_router_kernel, n_exp=n_exp),
        grid=(n // tm,),
        in_specs=[
            pl.BlockSpec((tm, d), lambda i: (i, 0)),
            pl.BlockSpec((n_exp, d), lambda i: (0, 0)),
            pl.BlockSpec((n_exp, 1), lambda i: (0, 0)),
        ],
        out_specs=[
            pl.BlockSpec((TOP_K, tm), lambda i: (0, i)),
            pl.BlockSpec((TOP_K, tm), lambda i: (0, i)),
        ],
        out_shape=[
            jax.ShapeDtypeStruct((TOP_K, n), jnp.int32),
            jax.ShapeDtypeStruct((TOP_K, n), F32),
        ],
        compiler_params=_params("arbitrary"),
        name="router",
    )(x, w_router_t, bias_col)


def _dispatch(idx, n_exp):
    k, n = idx.shape
    e = idx.reshape(-1)
    onehot = (e[:, None] == jnp.arange(n_exp, dtype=jnp.int32)[None, :]).astype(jnp.int32)
    cums = jnp.cumsum(onehot, axis=0)
    rank = jnp.take_along_axis(cums, e[:, None], axis=1)[:, 0] - 1
    counts = cums[-1]
    padded = (counts + MOE_ROWS - 1) // MOE_ROWS * MOE_ROWS
    pend = jnp.cumsum(padded)
    dest = (pend - padded)[e] + rank
    n_blocks = (k * n) // MOE_ROWS + n_exp
    tok = jnp.tile(jnp.arange(n, dtype=jnp.int32), k)
    row_tok = jnp.zeros((n_blocks * MOE_ROWS,), jnp.int32).at[dest].set(tok)
    blk_start = jnp.arange(n_blocks, dtype=jnp.int32) * MOE_ROWS
    blk_expert = jnp.minimum(
        jnp.sum((pend[None, :] <= blk_start[:, None]).astype(jnp.int32), axis=1), n_exp - 1)
    n_used = (pend[-1:] // MOE_ROWS).astype(jnp.int32)
    return dest.astype(jnp.int32), row_tok, blk_expert, n_used


def _start_row_gather(idx_ref, base, count, src_hbm, dst_ref, sem):
    for i in range(count):
        r = idx_ref[base + i]
        pltpu.make_async_copy(src_hbm.at[pl.ds(r, 1), :], dst_ref.at[pl.ds(i, 1), :], sem).start()


def _wait_row_gather(count, src_hbm, dst_ref, sem):
    pltpu.make_async_copy(src_hbm.at[pl.ds(0, count), :], dst_ref, sem).wait()


def _start_row_gather_loop(idx_ref, base, count, src_hbm, dst_ref, sem):
    def issue(i, c):
        r = idx_ref[base + i]
        pltpu.make_async_copy(src_hbm.at[pl.ds(r, 1), :], dst_ref.at[pl.ds(i, 1), :], sem).start()
        return c
    lax.fori_loop(0, count, issue, 0, unroll=8)


def _moe_up_kernel(blk_ref, tok_ref, nused_ref, xp_hbm, wg_ref, wu_ref, h_ref, xg0_ref, xg1_ref, sems):
    b = pl.program_id(0)
    n_used = nused_ref[0]
    half = wg_ref.shape[2] // 2

    @pl.when((b == 0) & (n_used > 0))
    def _():
        _start_row_gather_loop(tok_ref, 0, MOE_ROWS, xp_hbm, xg0_ref, sems.at[0])

    def step(cur_ref, cur_sem, nxt_ref, nxt_sem):
        _wait_row_gather(MOE_ROWS, xp_hbm, cur_ref, cur_sem)

        @pl.when(b + 1 < n_used)
        def _():
            _start_row_gather_loop(tok_ref, (b + 1) * MOE_ROWS, MOE_ROWS, xp_hbm, nxt_ref, nxt_sem)

        lo, hi = _unpack_bf16_pairs(cur_ref[...])
        g = (jnp.dot(lo, wg_ref[0, 0, :half, :], preferred_element_type=F32)
             + jnp.dot(hi, wg_ref[0, 0, half:, :], preferred_element_type=F32))
        u = (jnp.dot(lo, wu_ref[0, 0, :half, :], preferred_element_type=F32)
             + jnp.dot(hi, wu_ref[0, 0, half:, :], preferred_element_type=F32))
        h_ref[...] = (g * jax.nn.sigmoid(g) * u).astype(BF16)

    @pl.when((b < n_used) & (b % 2 == 0))
    def _():
        step(xg0_ref, sems.at[0], xg1_ref, sems.at[1])

    @pl.when((b < n_used) & (b % 2 == 1))
    def _():
        step(xg1_ref, sems.at[1], xg0_ref, sems.at[0])

    @pl.when(b >= n_used)
    def _():
        h_ref[...] = jnp.zeros_like(h_ref)


def _moe_up(xp, w_gate, w_up, layer, blk_expert, row_tok, n_used):
    n, dp = xp.shape
    n_blocks = blk_expert.shape[0]
    d, d_ff = w_gate.shape[-2:]
    return pl.pallas_call(
        _moe_up_kernel,
        grid_spec=pltpu.PrefetchScalarGridSpec(
            num_scalar_prefetch=3,
            grid=(n_blocks,),
            in_specs=[
                pl.BlockSpec(memory_space=pl.ANY),
                pl.BlockSpec((1, 1, d, d_ff), lambda b, blk, tok, nu: (layer, blk[b], 0, 0)),
                pl.BlockSpec((1, 1, d, d_ff), lambda b, blk, tok, nu: (layer, blk[b], 0, 0)),
            ],
            out_specs=pl.BlockSpec((MOE_ROWS, d_ff), lambda b, blk, tok, nu: (b, 0)),
            scratch_shapes=[pltpu.VMEM((MOE_ROWS, dp), jnp.uint32), pltpu.VMEM((MOE_ROWS, dp), jnp.uint32),
                            pltpu.SemaphoreType.DMA((2,))],
        ),
        out_shape=jax.ShapeDtypeStruct((n_blocks * MOE_ROWS, d_ff), BF16),
        compiler_params=_params("arbitrary"),
        name="moe_up",
    )(blk_expert, row_tok, n_used, xp, w_gate, w_up)


def _moe_down_kernel(blk_ref, nused_ref, h_ref, wd_ref, y_ref):
    b = pl.program_id(0)

    @pl.when(b < nused_ref[0])
    def _():
        y_ref[...] = jnp.dot(h_ref[...], wd_ref[0, 0], preferred_element_type=F32)

    @pl.when(b >= nused_ref[0])
    def _():
        y_ref[...] = jnp.zeros_like(y_ref)


def _moe_down(h, w_down, layer, blk_expert, n_used):
    n_blocks = blk_expert.shape[0]
    d_ff, d = w_down.shape[2:]
    return pl.pallas_call(
        _moe_down_kernel,
        grid_spec=pltpu.PrefetchScalarGridSpec(
            num_scalar_prefetch=2,
            grid=(n_blocks,),
            in_specs=[
                pl.BlockSpec((MOE_ROWS, d_ff), lambda b, blk, nu: (b, 0)),
                pl.BlockSpec((1, 1, d_ff, d), lambda b, blk, nu: (layer, blk[b], 0, 0)),
            ],
            out_specs=pl.BlockSpec((MOE_ROWS, d), lambda b, blk, nu: (b, 0)),
        ),
        out_shape=jax.ShapeDtypeStruct((n_blocks * MOE_ROWS, d), F32),
        compiler_params=_params("arbitrary"),
        name="moe_down",
    )(blk_expert, n_used, h, w_down)


def _combine_kernel(dest_ref, y_hbm, x_ref, wt_ref, g_ref, b_ref, o_ref, ob_ref,
                    ya0_ref, yb0_ref, ya1_ref, yb1_ref, sems, *, tc, n_tok, alpha):
    i = pl.program_id(0)
    last = pl.num_programs(0) - 1
    nxt_base = jnp.minimum(i + 1, last) * tc

    def start(base, ya_ref, yb_ref, slot):
        _start_row_gather(dest_ref, base, tc, y_hbm, ya_ref, sems.at[slot, 0])
        _start_row_gather(dest_ref, n_tok + base, tc, y_hbm, yb_ref, sems.at[slot, 1])

    def wait(ya_ref, yb_ref, slot):
        _wait_row_gather(tc, y_hbm, ya_ref, sems.at[slot, 0])
        _wait_row_gather(tc, y_hbm, yb_ref, sems.at[slot, 1])

    @pl.when(i == 0)
    def _():
        start(0, ya0_ref, yb0_ref, 0)

    def step(cur, nxt):
        wait(*cur)
        start(nxt_base, *nxt)
        moe = wt_ref[:, 0:1] * cur[0][...] + wt_ref[:, 1:2] * cur[1][...]
        out = _layer_norm_rows(alpha * x_ref[...] + moe, g_ref[...], b_ref[...])
        o_ref[...] = out
        ob_ref[...] = out.astype(BF16)

    slot0 = (ya0_ref, yb0_ref, 0)
    slot1 = (ya1_ref, yb1_ref, 1)

    @pl.when(i % 2 == 0)
    def _():
        step(slot0, slot1)

    @pl.when(i % 2 == 1)
    def _():
        step(slot1, slot0)

    @pl.when(i == last)
    def _():
        @pl.when(i % 2 == 0)
        def _():
            wait(*slot1)

        @pl.when(i % 2 == 1)
        def _():
            wait(*slot0)


def _combine_ln(y, x, wts_tok, dest, g, b, *, alpha):
    n, d = x.shape
    tc = min(256, n)
    kern = functools.partial(_combine_kernel, tc=tc, n_tok=n, alpha=alpha)
    return pl.pallas_call(
        kern,
        grid_spec=pltpu.PrefetchScalarGridSpec(
            num_scalar_prefetch=1,
            grid=(n // tc,),
            in_specs=[
                pl.BlockSpec(memory_space=pl.ANY),
                pl.BlockSpec((tc, d), lambda i, dest: (i, 0)),
                pl.BlockSpec((tc, TOP_K), lambda i, dest: (i, 0)),
                pl.BlockSpec((1, d), lambda i, dest: (0, 0)),
                pl.BlockSpec((1, d), lambda i, dest: (0, 0)),
            ],
            out_specs=[
                pl.BlockSpec((tc, d), lambda i, dest: (i, 0)),
                pl.BlockSpec((tc, d), lambda i, dest: (i, 0)),
            ],
            scratch_shapes=[pltpu.VMEM((tc, d), F32)] * 4 + [pltpu.SemaphoreType.DMA((2, 2))],
        ),
        out_shape=[jax.ShapeDtypeStruct((n, d), F32), jax.ShapeDtypeStruct((n, d), BF16)],
        compiler_params=_params("arbitrary"),
        name="combine_ln",
    )(dest, y, x, wts_tok, g, b)


def _rope_tables(seq):
    half = HEAD_DIM // 2
    inv = ROPE_THETA ** (-jnp.arange(half, dtype=F32) / half)
    ang = jnp.arange(seq, dtype=jnp.int32).astype(F32)[:, None] * inv[None, :]
    cos, sin = jnp.cos(ang), jnp.sin(ang)
    return jnp.concatenate([cos, cos], axis=-1), jnp.concatenate([-sin, sin], axis=-1)


def kernel(x, w_in, b_forget, w_out, ln1_g, ln1_b, w_router, router_bias, w_gate, w_up, w_down, ln2_g, ln2_b):
    batch, seq, d = x.shape
    depth = w_in.shape[0]
    fox_heads = b_forget.shape[-1]
    fox_w = fox_heads * HEAD_DIM
    moba_w = w_out.shape[1] - fox_w
    moba_heads = moba_w // HEAD_DIM
    n_exp = w_router.shape[1]
    n = batch * seq
    alpha = (2.0 * depth) ** 0.25
    qkv_w = 3 * (fox_w + moba_w)
    hb = HEAD_DIM

    cos, sin = _rope_tables(seq)
    w_router_t = w_router.T
    bias_col = router_bias.reshape(n_exp, 1)
    xf = x.reshape(n, d)
    xb = xf.astype(BF16)
    w_in_t = jnp.swapaxes(w_in, 1, 2)
    w_out_b = w_out.astype(BF16)
    w_gate_b, w_up_b, w_down_b = w_gate.astype(BF16), w_up.astype(BF16), w_down.astype(BF16)

    for l in range(depth):
        w_f = jnp.pad(w_in_t[l, qkv_w:, :], ((0, LANES - fox_heads), (0, 0)))
        b_f = jnp.pad(b_forget[l], (0, LANES - fox_heads)).reshape(1, LANES)

        proj, kbar = _in_proj(xb, w_in_t, l, cos, sin, fox_w=fox_w, moba_w=moba_w, seq=seq)
        cp = _forget_cumsum(xb, w_f, b_f, batch=batch, seq=seq, heads=fox_heads)
        o_f = _fox_attention(proj, cp, batch=batch, seq=seq, heads=fox_heads,
                             k_col=fox_w // hb, v_col=2 * fox_w // hb)
        o_m = _moba_attention(proj, kbar.reshape(batch, seq // MOBA_BLOCK, moba_w),
                              batch=batch, seq=seq, heads=moba_heads,
                              q_col=3 * fox_w // hb, k_col=(3 * fox_w + moba_w) // hb,
                              v_col=(3 * fox_w + 2 * moba_w) // hb)
        x1, x1p = _out_proj_ln(o_f, o_m, w_out_b, l, xf,
                               ln1_g[l].reshape(1, d), ln1_b[l].reshape(1, d), alpha=alpha)

        idx, wts = _router(x1, w_router_t, bias_col)
        dest, row_tok, blk_expert, n_used = _dispatch(idx, n_exp)
        h = _moe_up(x1p, w_gate_b, w_up_b, l, blk_expert, row_tok, n_used)
        y = _moe_down(h, w_down_b, l, blk_expert, n_used)
        xf, xb = _combine_ln(y, x1, wts.T, dest, ln2_g[l].reshape(1, d), ln2_b[l].reshape(1, d),
                             alpha=alpha)

    return xf.reshape(batch, seq, d)
```

```python
import functools
import math

import jax
import jax.numpy as jnp
from jax import lax
from jax.experimental import pallas as pl
from jax.experimental.pallas import tpu as pltpu

HEAD_DIM = 128
MOBA_BLOCK = 256
MOBA_TOPK = 3
ROPE_THETA = 10000.0
N_GROUPS = 4
TOP_K = 2
LN_EPS = 1e-5
NEG_INF = -1e30
LOG2E = math.log2(math.e)
Q_SCALE = HEAD_DIM ** -0.5 * LOG2E
MOE_ROWS = 256
LANES = 128
VMEM_LIMIT = 56 * 1024 * 1024
ATTN_TILE = 2048
ATTN_KEY_TILE = 1024
ATTN_SUB = 256

F32 = jnp.float32
BF16 = jnp.bfloat16
NT_DIMS = (((1,), (1,)), ((), ()))


def _params(*sem):
    return pltpu.CompilerParams(dimension_semantics=sem, vmem_limit_bytes=VMEM_LIMIT)


def _split3(a):
    hi = a.astype(BF16)
    r1 = a - hi.astype(F32)
    mid = r1.astype(BF16)
    lo = (r1 - mid.astype(F32)).astype(BF16)
    return hi, mid, lo


def _in_proj_kernel(x_ref, w_ref, cos_ref, sin_ref, o_ref, kbar_ref, *, tm, tn, j0, j1, j2, j3):
    acc = lax.dot_general(x_ref[...], w_ref[0].astype(BF16), NT_DIMS, preferred_element_type=F32)
    j = pl.program_id(1)

    @pl.when(j < j0)
    def _():
        o_ref[...] = (acc * Q_SCALE).astype(BF16)

    @pl.when(((j >= j0) & (j < j1)) | (j >= j3))
    def _():
        o_ref[...] = acc.astype(BF16)

    def roped(c):
        ch = acc[:, c * HEAD_DIM:(c + 1) * HEAD_DIM]
        return ch * cos_ref[...] + pltpu.roll(ch, HEAD_DIM // 2, 1) * sin_ref[...]

    @pl.when((j >= j1) & (j < j2))
    def _():
        for c in range(tn // HEAD_DIM):
            o_ref[:, c * HEAD_DIM:(c + 1) * HEAD_DIM] = (roped(c) * Q_SCALE).astype(BF16)

    @pl.when((j >= j2) & (j < j3))
    def _():
        for c in range(tn // HEAD_DIM):
            r = roped(c)
            o_ref[:, c * HEAD_DIM:(c + 1) * HEAD_DIM] = r.astype(BF16)
            kbar_ref[0, :, c * HEAD_DIM:(c + 1) * HEAD_DIM] = (
                r.reshape(tm // MOBA_BLOCK, MOBA_BLOCK, HEAD_DIM).sum(axis=1) * (1.0 / MOBA_BLOCK))


def _in_proj(xb, w_in_t, layer, cos, sin, *, fox_w, moba_w, seq):
    n, d = xb.shape
    tm = min(1024, seq)
    tn = min(512, fox_w, moba_w)
    fq, mq = fox_w // tn, moba_w // tn
    j0, j1 = fq, 3 * fq
    j2, j3 = j1 + mq, j1 + 2 * mq
    nj = j1 + 3 * mq
    spb = seq // tm
    kern = functools.partial(_in_proj_kernel, tm=tm, tn=tn, j0=j0, j1=j1, j2=j2, j3=j3)
    return pl.pallas_call(
        kern,
        grid=(n // tm, nj),
        in_specs=[
            pl.BlockSpec((tm, d), lambda i, j: (i, 0)),
            pl.BlockSpec((1, tn, d), lambda i, j: (layer, j, 0)),
            pl.BlockSpec((tm, HEAD_DIM), lambda i, j: (i % spb, 0)),
            pl.BlockSpec((tm, HEAD_DIM), lambda i, j: (i % spb, 0)),
        ],
        out_specs=[
            pl.BlockSpec((tm, tn), lambda i, j: (i, j)),
            pl.BlockSpec((1, tm // MOBA_BLOCK, tn),
                         lambda i, j: (i, 0, jnp.clip(j - j2, 0, mq - 1))),
        ],
        out_shape=[
            jax.ShapeDtypeStruct((n, 3 * (fox_w + moba_w)), BF16),
            jax.ShapeDtypeStruct((n // tm, tm // MOBA_BLOCK, moba_w), F32),
        ],
        compiler_params=_params("arbitrary", "arbitrary"),
        name="in_proj",
    )(xb, w_in_t, cos, sin)


def _forget_kernel(x_ref, w_ref, b_ref, cp_ref, carry_ref, *, ts, heads):
    @pl.when(pl.program_id(1) == 0)
    def _():
        carry_ref[...] = jnp.zeros_like(carry_ref)

    f = lax.dot_general(x_ref[...], w_ref[...].astype(BF16), NT_DIMS, preferred_element_type=F32) + b_ref[...]
    lf = jnp.minimum(f, 0.0) - jnp.log1p(jnp.exp(-jnp.abs(f)))
    row = lax.broadcasted_iota(jnp.int32, (ts, ts), 0)
    col = lax.broadcasted_iota(jnp.int32, (ts, ts), 1)
    tri = jnp.where(row >= col, 1.0, 0.0).astype(BF16)
    cs = carry_ref[...]
    for part in _split3(lf):
        cs = cs + jnp.dot(tri, part, preferred_element_type=F32)
    carry_ref[...] = cs[ts - 1:ts, :]

    lane = lax.broadcasted_iota(jnp.int32, (ts, LANES), 1)
    neg = jnp.where(lane < heads, -LOG2E * cs, 0.0)
    erow = lax.broadcasted_iota(jnp.int32, (LANES, LANES), 0)
    ecol = lax.broadcasted_iota(jnp.int32, (LANES, LANES), 1)
    out = jnp.zeros((ts, LANES), F32)
    for j, part in enumerate(_split3(neg)):
        spread = jnp.where(ecol == 3 * erow + j, 1.0, 0.0).astype(BF16)
        out = out + jnp.dot(part, spread, preferred_element_type=F32)
    cp_ref[...] = out.astype(BF16)


def _forget_cumsum(xb, w_f, b_f, *, batch, seq, heads):
    n, d = xb.shape
    assert 3 * heads <= LANES
    ts = min(512, seq)
    spb = seq // ts
    return pl.pallas_call(
        functools.partial(_forget_kernel, ts=ts, heads=heads),
        grid=(batch, spb),
        in_specs=[
            pl.BlockSpec((ts, d), lambda b, s: (b * spb + s, 0)),
            pl.BlockSpec((LANES, d), lambda b, s: (0, 0)),
            pl.BlockSpec((1, LANES), lambda b, s: (0, 0)),
        ],
        out_specs=pl.BlockSpec((ts, LANES), lambda b, s: (b * spb + s, 0)),
        out_shape=jax.ShapeDtypeStruct((n, LANES), BF16),
        scratch_shapes=[pltpu.VMEM((1, LANES), F32)],
        compiler_params=_params("arbitrary", "arbitrary"),
        name="forget_cumsum",
    )(xb, w_f, b_f)


def _softmax_update(m, l, s_t):
    m_new = jnp.maximum(m, s_t.max(axis=0, keepdims=True))
    alpha = jnp.exp2(m - m_new)
    p_t = jnp.exp2(s_t - m_new)
    return m_new, alpha * l + p_t.sum(axis=0, keepdims=True), alpha, p_t.astype(BF16)


LEAD = 4


def _attend_tile(q_subs, keys, values_t, masks, carry):
    n_sub = len(q_subs)

    def scores(r):
        s_t = jnp.dot(keys[r], q_subs[r], preferred_element_type=F32)
        return s_t if masks[r] is None else jnp.where(masks[r], s_t, NEG_INF)

    s = {r: scores(r) for r in range(min(LEAD, n_sub))}
    out = []
    for r in range(n_sub):
        m, l, acc_t = carry[r]
        m, l, alpha, p_t = _softmax_update(m, l, s.pop(r))
        if r + LEAD < n_sub:
            s[r + LEAD] = scores(r + LEAD)
        acc_t = alpha * acc_t + jnp.dot(values_t[r], p_t, preferred_element_type=F32)
        out.append((m, l, acc_t))
    return tuple(out)


def _transpose_bf16(a):
    return a.astype(F32).T.astype(BF16)


def _store_keys_values(k_ref, v_ref, kaug_ref, vt_ref, aug, *, chunk):
    seq = k_ref.shape[0]
    kaug_ref[:, :HEAD_DIM] = k_ref[...]
    kaug_ref[:, HEAD_DIM:] = aug
    for c in range(seq // chunk):
        vt_ref[:, c * chunk:(c + 1) * chunk] = _transpose_bf16(v_ref[c * chunk:(c + 1) * chunk, :])


def _flash_attention(qt_aug, kaug_ref, vt_ref, o_ref, qi, *, t, sub):
    n_sub = t // sub
    q_subs = [qt_aug[:, r * sub:(r + 1) * sub] for r in range(n_sub)]

    tk = min(ATTN_KEY_TILE, t)

    def past(ki, carry):
        start = pl.multiple_of(ki * tk, tk)
        k = kaug_ref[pl.ds(start, tk), :]
        v_t = vt_ref[:, pl.ds(start, tk)]
        return _attend_tile(q_subs, [k] * n_sub, [v_t] * n_sub, [None] * n_sub, carry)

    init = tuple((jnp.full((1, sub), -jnp.inf, F32), jnp.zeros((1, sub), F32), jnp.zeros((HEAD_DIM, sub), F32))
                 for _ in range(n_sub))
    carry = lax.fori_loop(0, qi * (t // tk), past, init)

    start = pl.multiple_of(qi * t, t)
    keys, values_t, masks = [], [], []
    for r in range(n_sub):
        width = (r + 1) * sub
        keys.append(kaug_ref[pl.ds(start, width), :])
        values_t.append(vt_ref[:, pl.ds(start, width)])
        key = lax.broadcasted_iota(jnp.int32, (width, sub), 0)
        qry = lax.broadcasted_iota(jnp.int32, (width, sub), 1) + r * sub
        masks.append(key <= qry)
    carry = _attend_tile(q_subs, keys, values_t, masks, carry)
    for r in range(n_sub):
        _, l, acc_t = carry[r]
        o_ref[r * sub:(r + 1) * sub, :] = (acc_t / l).T.astype(BF16)


def _fox_kernel(q_ref, k_ref, v_ref, cp_ref, o_ref, kaug_ref, vt_ref, *, t, sub):
    h = pl.program_id(1)
    qi = pl.program_id(2)

    @pl.when(qi == 0)
    def _():
        lane = lax.broadcasted_iota(jnp.int32, cp_ref.shape, 1)
        mine = (lane >= 3 * h) & (lane < 3 * h + 3)
        _store_keys_values(k_ref, v_ref, kaug_ref, vt_ref,
                           jnp.where(mine, cp_ref[...], jnp.zeros_like(cp_ref)), chunk=t)

    row = lax.broadcasted_iota(jnp.int32, (LANES, t), 0)
    ones_t = jnp.where((row >= 3 * h) & (row < 3 * h + 3), 1.0, 0.0).astype(BF16)
    qt_aug = jnp.concatenate([_transpose_bf16(q_ref[...]), ones_t], axis=0)
    _flash_attention(qt_aug, kaug_ref, vt_ref, o_ref, qi, t=t, sub=sub)


def _fox_attention(proj, cp, *, batch, seq, heads, k_col, v_col):
    n = proj.shape[0]
    t = min(ATTN_TILE, seq)
    sub = min(ATTN_SUB, t)
    nq = seq // t
    return pl.pallas_call(
        functools.partial(_fox_kernel, t=t, sub=sub),
        grid=(batch, heads, nq),
        in_specs=[
            pl.BlockSpec((t, HEAD_DIM), lambda b, h, qi: (b * nq + qi, h)),
            pl.BlockSpec((seq, HEAD_DIM), lambda b, h, qi: (b, k_col + h)),
            pl.BlockSpec((seq, HEAD_DIM), lambda b, h, qi: (b, v_col + h)),
            pl.BlockSpec((seq, LANES), lambda b, h, qi: (b, 0)),
        ],
        out_specs=pl.BlockSpec((t, HEAD_DIM), lambda b, h, qi: (b * nq + qi, h)),
        out_shape=jax.ShapeDtypeStruct((n, heads * HEAD_DIM), BF16),
        scratch_shapes=[pltpu.VMEM((seq, 2 * HEAD_DIM), BF16), pltpu.VMEM((HEAD_DIM, seq), BF16)],
        compiler_params=_params("arbitrary", "arbitrary", "arbitrary"),
        name="fox_attention",
    )(proj, proj, proj, cp)


def _moba_kernel(q_ref, k_ref, v_ref, kbar_ref, o_ref, kaug_ref, vt_ref, *, t, sub, nb):
    qi = pl.program_id(2)
    seq = k_ref.shape[0]

    @pl.when(qi == 0)
    def _():
        r = lax.broadcasted_iota(jnp.int32, (seq, LANES), 0)
        c = lax.broadcasted_iota(jnp.int32, (seq, LANES), 1)
        _store_keys_values(k_ref, v_ref, kaug_ref, vt_ref,
                           jnp.where(r // MOBA_BLOCK == c, 1.0, 0.0).astype(BF16), chunk=t)

    q = q_ref[...]
    gate = jnp.zeros((nb, t), F32)
    for part in _split3(kbar_ref[0]):
        gate = gate + lax.dot_general(part, q, NT_DIMS, preferred_element_type=F32)
    blk = lax.broadcasted_iota(jnp.int32, (nb, t), 0)
    own = qi * (t // MOBA_BLOCK) + lax.broadcasted_iota(jnp.int32, (nb, t), 1) // MOBA_BLOCK
    valid = blk < own
    g = jnp.where(valid, gate, NEG_INF)
    rank = jnp.zeros((nb, t), jnp.int32)
    for m in range(nb):
        gm = g[m:m + 1, :]
        beats = (gm > g) | ((gm == g) & (blk > m))
        rank = rank + jnp.where(beats, 1, 0)
    allowed = (valid & (rank < MOBA_TOPK)) | (blk == own)
    bias_t = jnp.where(allowed, 0.0, NEG_INF).astype(BF16)
    qt_aug = jnp.concatenate([_transpose_bf16(q), bias_t, jnp.zeros((LANES - nb, t), BF16)], axis=0)
    _flash_attention(qt_aug, kaug_ref, vt_ref, o_ref, qi, t=t, sub=sub)


def _moba_attention(proj, kbar, *, batch, seq, heads, q_col, k_col, v_col):
    n = proj.shape[0]
    nb = seq // MOBA_BLOCK
    assert nb <= LANES
    t = min(ATTN_TILE, seq)
    nq = seq // t
    return pl.pallas_call(
        functools.partial(_moba_kernel, t=t, sub=min(ATTN_SUB, t), nb=nb),
        grid=(batch, heads, nq),
        in_specs=[
            pl.BlockSpec((t, HEAD_DIM), lambda b, h, qi: (b * nq + qi, q_col + h)),
            pl.BlockSpec((seq, HEAD_DIM), lambda b, h, qi: (b, k_col + h)),
            pl.BlockSpec((seq, HEAD_DIM), lambda b, h, qi: (b, v_col + h)),
            pl.BlockSpec((1, nb, HEAD_DIM), lambda b, h, qi: (b, 0, h)),
        ],
        out_specs=pl.BlockSpec((t, HEAD_DIM), lambda b, h, qi: (b * nq + qi, h)),
        out_shape=jax.ShapeDtypeStruct((n, heads * HEAD_DIM), BF16),
        scratch_shapes=[pltpu.VMEM((seq, 2 * HEAD_DIM), BF16), pltpu.VMEM((HEAD_DIM, seq), BF16)],
        compiler_params=_params("arbitrary", "arbitrary", "arbitrary"),
        name="moba_attention",
    )(proj, proj, proj, kbar)


def _layer_norm_rows(z, g, b):
    mu = jnp.mean(z, axis=-1, keepdims=True)
    zc = z - mu
    var = jnp.mean(zc * zc, axis=-1, keepdims=True)
    return zc * lax.rsqrt(var + LN_EPS) * g + b


def _pack_bf16_pairs(x):
    half = x.shape[1] // 2
    lo = pltpu.bitcast(x[:, :half].astype(BF16).astype(F32), jnp.uint32)
    hi = pltpu.bitcast(x[:, half:].astype(BF16).astype(F32), jnp.uint32)
    return (lo >> 16) | (hi & jnp.uint32(0xFFFF0000))


def _unpack_bf16_pairs(p):
    lo = pltpu.bitcast(p << 16, F32).astype(BF16)
    hi = pltpu.bitcast(p & jnp.uint32(0xFFFF0000), F32).astype(BF16)
    return lo, hi


def _out_proj_kernel(of_ref, om_ref, w_ref, x_ref, g_ref, b_ref, o_ref, op_ref, *, tn, fox_w, alpha):
    j = pl.program_id(1)
    acc = jnp.dot(of_ref[...], w_ref[0, :fox_w, :], preferred_element_type=F32)
    acc = acc + jnp.dot(om_ref[...], w_ref[0, fox_w:, :], preferred_element_type=F32)
    o_ref[:, pl.ds(pl.multiple_of(j * tn, tn), tn)] = alpha * x_ref[...] + acc

    @pl.when(j == pl.num_programs(1) - 1)
    def _():
        out = _layer_norm_rows(o_ref[...], g_ref[...], b_ref[...])
        o_ref[...] = out
        op_ref[...] = _pack_bf16_pairs(out)


def _out_proj_ln(o_f, o_m, w, layer, x, g, b, *, alpha):
    n, d = x.shape
    fox_w, moba_w = o_f.shape[1], o_m.shape[1]
    tm = min(512, n)
    tn = min(512, d)
    kern = functools.partial(_out_proj_kernel, tn=tn, fox_w=fox_w, alpha=alpha)
    return pl.pallas_call(
        kern,
        grid=(n // tm, d // tn),
        in_specs=[
            pl.BlockSpec((tm, fox_w), lambda i, j: (i, 0)),
            pl.BlockSpec((tm, moba_w), lambda i, j: (i, 0)),
            pl.BlockSpec((1, fox_w + moba_w, tn), lambda i, j: (layer, 0, j)),
            pl.BlockSpec((tm, tn), lambda i, j: (i, j)),
            pl.BlockSpec((1, d), lambda i, j: (0, 0)),
            pl.BlockSpec((1, d), lambda i, j: (0, 0)),
        ],
        out_specs=[pl.BlockSpec((tm, d), lambda i, j: (i, 0)),
                   pl.BlockSpec((tm, d // 2), lambda i, j: (i, 0))],
        out_shape=[jax.ShapeDtypeStruct((n, d), F32), jax.ShapeDtypeStruct((n, d // 2), jnp.uint32)],
        compiler_params=_params("arbitrary", "arbitrary"),
        name="out_proj_ln",
    )(o_f, o_m, w, x, g, b)


def _router_kernel(x_ref, wt_ref, bias_ref, idx_ref, wts_ref, *, n_exp):
    x = x_ref[...]
    x_hi = x.astype(BF16)
    x_lo = (x - x_hi.astype(F32)).astype(BF16)
    w = wt_ref[...]
    w_hi = w.astype(BF16)
    w_lo = (w - w_hi.astype(F32)).astype(BF16)
    part = lax.dot_general(jnp.concatenate([w_hi, w_lo], axis=0), x_hi, NT_DIMS, preferred_element_type=F32)
    logits = (part[:n_exp] + part[n_exp:]
              + lax.dot_general(w_hi, x_lo, NT_DIMS, preferred_element_type=F32))
    ex = jnp.exp(logits - logits.max(axis=0, keepdims=True))
    probs = ex / ex.sum(axis=0, keepdims=True)
    sel = probs + bias_ref[...]
    per = n_exp // N_GROUPS
    gscore = []
    for gidx in range(N_GROUPS):
        r = [sel[gidx * per + a:gidx * per + a + 1, :] for a in range(per)]
        best = None
        for a in range(per):
            for b in range(a + 1, per):
                pair = r[a] + r[b]
                best = pair if best is None else jnp.maximum(best, pair)
        gscore.append(best)
    gmax = functools.reduce(jnp.maximum, gscore)
    g_idx = jnp.full_like(gmax, N_GROUPS - 1).astype(jnp.int32)
    for gidx in range(N_GROUPS - 2, -1, -1):
        g_idx = jnp.where(gscore[gidx] == gmax, gidx, g_idx)
    erow = lax.broadcasted_iota(jnp.int32, sel.shape, 0)
    masked = jnp.where(erow // per == g_idx, sel, NEG_INF)
    v1 = masked.max(axis=0, keepdims=True)
    i1 = jnp.min(jnp.where(masked == v1, erow, n_exp), axis=0, keepdims=True)
    masked2 = jnp.where(erow == i1, -jnp.inf, masked)
    v2 = masked2.max(axis=0, keepdims=True)
    i2 = jnp.min(jnp.where(masked2 == v2, erow, n_exp), axis=0, keepdims=True)
    w1 = jnp.sum(jnp.where(erow == i1, probs, 0.0), axis=0, keepdims=True)
    w2 = jnp.sum(jnp.where(erow == i2, probs, 0.0), axis=0, keepdims=True)
    tot = w1 + w2
    idx_ref[...] = jnp.concatenate([i1, i2], axis=0)
    wts_ref[...] = jnp.concatenate([w1 / tot, w2 / tot], axis=0)


def _router(x, w_router_t, bias_col):
    n, d = x.shape
    n_exp = w_router_t.shape[0]
    tm = min(512, n)
    return pl.pallas_call(
        functools.partial(_router_kernel, n_exp=n_exp),
        grid=(n // tm,),
        in_specs=[
            pl.BlockSpec((tm, d), lambda i: (i, 0)),
            pl.BlockSpec((n_exp, d), lambda i: (0, 0)),
            pl.BlockSpec((n_exp, 1), lambda i: (0, 0)),
        ],
        out_specs=[
            pl.BlockSpec((TOP_K, tm), lambda i: (0, i)),
            pl.BlockSpec((TOP_K, tm), lambda i: (0, i)),
        ],
        out_shape=[
            jax.ShapeDtypeStruct((TOP_K, n), jnp.int32),
            jax.ShapeDtypeStruct((TOP_K, n), F32),
        ],
        compiler_params=_params("arbitrary"),
        name="router",
    )(x, w_router_t, bias_col)


def _dispatch(idx, n_exp):
    k, n = idx.shape
    e = idx.reshape(-1)
    onehot = (e[:, None] == jnp.arange(n_exp, dtype=jnp.int32)[None, :]).astype(jnp.int32)
    cums = jnp.cumsum(onehot, axis=0)
    rank = jnp.take_along_axis(cums, e[:, None], axis=1)[:, 0] - 1
    counts = cums[-1]
    padded = (counts + MOE_ROWS - 1) // MOE_ROWS * MOE_ROWS
    pend = jnp.cumsum(padded)
    dest = (pend - padded)[e] + rank
    n_blocks = (k * n) // MOE_ROWS + n_exp
    tok = jnp.tile(jnp.arange(n, dtype=jnp.int32), k)
    row_tok = jnp.zeros((n_blocks * MOE_ROWS,), jnp.int32).at[dest].set(tok)
    blk_start = jnp.arange(n_blocks, dtype=jnp.int32) * MOE_ROWS
    blk_expert = jnp.minimum(
        jnp.sum((pend[None, :] <= blk_start[:, None]).astype(jnp.int32), axis=1), n_exp - 1)
    n_used = (pend[-1:] // MOE_ROWS).astype(jnp.int32)
    return dest.astype(jnp.int32), row_tok, blk_expert, n_used


def _start_row_gather(idx_ref, base, count, src_hbm, dst_ref, sem):
    for i in range(count):
        r = idx_ref[base + i]
        pltpu.make_async_copy(src_hbm.at[pl.ds(r, 1), :], dst_ref.at[pl.ds(i, 1), :], sem).start()


def _wait_row_gather(count, src_hbm, dst_ref, sem):
    pltpu.make_async_copy(src_hbm.at[pl.ds(0, count), :], dst_ref, sem).wait()


def _start_row_gather_loop(idx_ref, base, count, src_hbm, dst_ref, sem):
    def issue(i, c):
        r = idx_ref[base + i]
        pltpu.make_async_copy(src_hbm.at[pl.ds(r, 1), :], dst_ref.at[pl.ds(i, 1), :], sem).start()
        return c
    lax.fori_loop(0, count, issue, 0, unroll=8)


def _moe_up_kernel(blk_ref, tok_ref, nused_ref, xp_hbm, wg_ref, wu_ref, h_ref, xg0_ref, xg1_ref, sems):
    b = pl.program_id(0)
    n_used = nused_ref[0]
    half = wg_ref.shape[2] // 2

    @pl.when((b == 0) & (n_used > 0))
    def _():
        _start_row_gather_loop(tok_ref, 0, MOE_ROWS, xp_hbm, xg0_ref, sems.at[0])

    def step(cur_ref, cur_sem, nxt_ref, nxt_sem):
        _wait_row_gather(MOE_ROWS, xp_hbm, cur_ref, cur_sem)

        @pl.when(b + 1 < n_used)
        def _():
            _start_row_gather_loop(tok_ref, (b + 1) * MOE_ROWS, MOE_ROWS, xp_hbm, nxt_ref, nxt_sem)

        lo, hi = _unpack_bf16_pairs(cur_ref[...])
        g = (jnp.dot(lo, wg_ref[0, 0, :half, :], preferred_element_type=F32)
             + jnp.dot(hi, wg_ref[0, 0, half:, :], preferred_element_type=F32))
        u = (jnp.dot(lo, wu_ref[0, 0, :half, :], preferred_element_type=F32)
             + jnp.dot(hi, wu_ref[0, 0, half:, :], preferred_element_type=F32))
        h_ref[...] = (g * jax.nn.sigmoid(g) * u).astype(BF16)

    @pl.when((b < n_used) & (b % 2 == 0))
    def _():
        step(xg0_ref, sems.at[0], xg1_ref, sems.at[1])

    @pl.when((b < n_used) & (b % 2 == 1))
    def _():
        step(xg1_ref, sems.at[1], xg0_ref, sems.at[0])

    @pl.when(b >= n_used)
    def _():
        h_ref[...] = jnp.zeros_like(h_ref)


def _moe_up(xp, w_gate, w_up, layer, blk_expert, row_tok, n_used):
    n, dp = xp.shape
    n_blocks = blk_expert.shape[0]
    d, d_ff = w_gate.shape[-2:]
    return pl.pallas_call(
        _moe_up_kernel,
        grid_spec=pltpu.PrefetchScalarGridSpec(
            num_scalar_prefetch=3,
            grid=(n_blocks,),
            in_specs=[
                pl.BlockSpec(memory_space=pl.ANY),
                pl.BlockSpec((1, 1, d, d_ff), lambda b, blk, tok, nu: (layer, blk[b], 0, 0)),
                pl.BlockSpec((1, 1, d, d_ff), lambda b, blk, tok, nu: (layer, blk[b], 0, 0)),
            ],
            out_specs=pl.BlockSpec((MOE_ROWS, d_ff), lambda b, blk, tok, nu: (b, 0)),
            scratch_shapes=[pltpu.VMEM((MOE_ROWS, dp), jnp.uint32), pltpu.VMEM((MOE_ROWS, dp), jnp.uint32),
                            pltpu.SemaphoreType.DMA((2,))],
        ),
        out_shape=jax.ShapeDtypeStruct((n_blocks * MOE_ROWS, d_ff), BF16),
        compiler_params=_params("arbitrary"),
        name="moe_up",
    )(blk_expert, row_tok, n_used, xp, w_gate, w_up)


def _moe_down_kernel(blk_ref, nused_ref, h_ref, wd_ref, y_ref, wb_ref):
    b = pl.program_id(0)
    new_expert = (b == 0) | (blk_ref[b] != blk_ref[jnp.maximum(b - 1, 0)])

    @pl.when((b < nused_ref[0]) & new_expert)
    def _():
        wb_ref[...] = wd_ref[0, 0].astype(BF16)

    @pl.when(b < nused_ref[0])
    def _():
        y_ref[...] = jnp.dot(h_ref[...], wb_ref[...], preferred_element_type=F32)

    @pl.when(b >= nused_ref[0])
    def _():
        y_ref[...] = jnp.zeros_like(y_ref)


def _moe_down(h, w_down, layer, blk_expert, n_used):
    n_blocks = blk_expert.shape[0]
    d_ff, d = w_down.shape[2:]
    return pl.pallas_call(
        _moe_down_kernel,
        grid_spec=pltpu.PrefetchScalarGridSpec(
            num_scalar_prefetch=2,
            grid=(n_blocks,),
            in_specs=[
                pl.BlockSpec((MOE_ROWS, d_ff), lambda b, blk, nu: (b, 0)),
                pl.BlockSpec((1, 1, d_ff, d), lambda b, blk, nu: (layer, blk[b], 0, 0)),
            ],
            out_specs=pl.BlockSpec((MOE_ROWS, d), lambda b, blk, nu: (b, 0)),
            scratch_shapes=[pltpu.VMEM((d_ff, d), BF16)],
        ),
        out_shape=jax.ShapeDtypeStruct((n_blocks * MOE_ROWS, d), F32),
        compiler_params=_params("arbitrary"),
        name="moe_down",
    )(blk_expert, n_used, h, w_down)


def _combine_kernel(dest_ref, y_hbm, x_ref, wt_ref, g_ref, b_ref, o_ref, ob_ref,
                    ya0_ref, yb0_ref, ya1_ref, yb1_ref, sems, *, tc, n_tok, alpha):
    i = pl.program_id(0)
    last = pl.num_programs(0) - 1
    nxt_base = jnp.minimum(i + 1, last) * tc

    def start(base, ya_ref, yb_ref, slot):
        _start_row_gather(dest_ref, base, tc, y_hbm, ya_ref, sems.at[slot, 0])
        _start_row_gather(dest_ref, n_tok + base, tc, y_hbm, yb_ref, sems.at[slot, 1])

    def wait(ya_ref, yb_ref, slot):
        _wait_row_gather(tc, y_hbm, ya_ref, sems.at[slot, 0])
        _wait_row_gather(tc, y_hbm, yb_ref, sems.at[slot, 1])

    @pl.when(i == 0)
    def _():
        start(0, ya0_ref, yb0_ref, 0)

    def step(cur, nxt):
        wait(*cur)
        start(nxt_base, *nxt)
        moe = wt_ref[:, 0:1] * cur[0][...] + wt_ref[:, 1:2] * cur[1][...]
        out = _layer_norm_rows(alpha * x_ref[...] + moe, g_ref[...], b_ref[...])
        o_ref[...] = out
        ob_ref[...] = out.astype(BF16)

    slot0 = (ya0_ref, yb0_ref, 0)
    slot1 = (ya1_ref, yb1_ref, 1)

    @pl.when(i % 2 == 0)
    def _():
        step(slot0, slot1)

    @pl.when(i % 2 == 1)
    def _():
        step(slot1, slot0)

    @pl.when(i == last)
    def _():
        @pl.when(i % 2 == 0)
        def _():
            wait(*slot1)

        @pl.when(i % 2 == 1)
        def _():
            wait(*slot0)


def _combine_ln(y, x, wts_tok, dest, g, b, *, alpha):
    n, d = x.shape
    tc = min(256, n)
    kern = functools.partial(_combine_kernel, tc=tc, n_tok=n, alpha=alpha)
    return pl.pallas_call(
        kern,
        grid_spec=pltpu.PrefetchScalarGridSpec(
            num_scalar_prefetch=1,
            grid=(n // tc,),
            in_specs=[
                pl.BlockSpec(memory_space=pl.ANY),
                pl.BlockSpec((tc, d), lambda i, dest: (i, 0)),
                pl.BlockSpec((tc, TOP_K), lambda i, dest: (i, 0)),
                pl.BlockSpec((1, d), lambda i, dest: (0, 0)),
                pl.BlockSpec((1, d), lambda i, dest: (0, 0)),
            ],
            out_specs=[
                pl.BlockSpec((tc, d), lambda i, dest: (i, 0)),
                pl.BlockSpec((tc, d), lambda i, dest: (i, 0)),
            ],
            scratch_shapes=[pltpu.VMEM((tc, d), F32)] * 4 + [pltpu.SemaphoreType.DMA((2, 2))],
        ),
        out_shape=[jax.ShapeDtypeStruct((n, d), F32), jax.ShapeDtypeStruct((n, d), BF16)],
        compiler_params=_params("arbitrary"),
        name="combine_ln",
    )(dest, y, x, wts_tok, g, b)


def _rope_tables(seq):
    half = HEAD_DIM // 2
    inv = ROPE_THETA ** (-jnp.arange(half, dtype=F32) / half)
    ang = jnp.arange(seq, dtype=jnp.int32).astype(F32)[:, None] * inv[None, :]
    cos, sin = jnp.cos(ang), jnp.sin(ang)
    return jnp.concatenate([cos, cos], axis=-1), jnp.concatenate([-sin, sin], axis=-1)


def kernel(x, w_in, b_forget, w_out, ln1_g, ln1_b, w_router, router_bias, w_gate, w_up, w_down, ln2_g, ln2_b):
    batch, seq, d = x.shape
    depth = w_in.shape[0]
    fox_heads = b_forget.shape[-1]
    fox_w = fox_heads * HEAD_DIM
    moba_w = w_out.shape[1] - fox_w
    moba_heads = moba_w // HEAD_DIM
    n_exp = w_router.shape[1]
    n = batch * seq
    alpha = (2.0 * depth) ** 0.25
    qkv_w = 3 * (fox_w + moba_w)
    hb = HEAD_DIM

    cos, sin = _rope_tables(seq)
    w_router_t = w_router.T
    bias_col = router_bias.reshape(n_exp, 1)
    xf = x.reshape(n, d)
    xb = xf.astype(BF16)
    w_in_t = jnp.swapaxes(w_in, 1, 2)
    w_out_b = w_out.astype(BF16)
    w_gate_b, w_up_b = w_gate.astype(BF16), w_up.astype(BF16)

    for l in range(depth):
        w_f = jnp.pad(w_in_t[l, qkv_w:, :], ((0, LANES - fox_heads), (0, 0)))
        b_f = jnp.pad(b_forget[l], (0, LANES - fox_heads)).reshape(1, LANES)

        proj, kbar = _in_proj(xb, w_in_t, l, cos, sin, fox_w=fox_w, moba_w=moba_w, seq=seq)
        cp = _forget_cumsum(xb, w_f, b_f, batch=batch, seq=seq, heads=fox_heads)
        o_f = _fox_attention(proj, cp, batch=batch, seq=seq, heads=fox_heads,
                             k_col=fox_w // hb, v_col=2 * fox_w // hb)
        o_m = _moba_attention(proj, kbar.reshape(batch, seq // MOBA_BLOCK, moba_w),
                              batch=batch, seq=seq, heads=moba_heads,
                              q_col=3 * fox_w // hb, k_col=(3 * fox_w + moba_w) // hb,
                              v_col=(3 * fox_w + 2 * moba_w) // hb)
        x1, x1p = _out_proj_ln(o_f, o_m, w_out_b, l, xf,
                               ln1_g[l].reshape(1, d), ln1_b[l].reshape(1, d), alpha=alpha)

        idx, wts = _router(x1, w_router_t, bias_col)
        dest, row_tok, blk_expert, n_used = _dispatch(idx, n_exp)
        h = _moe_up(x1p, w_gate_b, w_up_b, l, blk_expert, row_tok, n_used)
        y = _moe_down(h, w_down, l, blk_expert, n_used)
        xf, xb = _combine_ln(y, x1, wts.T, dest, ln2_g[l].reshape(1, d), ln2_b[l].reshape(1, d),
                             alpha=alpha)

    return xf.reshape(batch, seq, d)
```

```python
import functools
import math

import jax
import jax.numpy as jnp
from jax import lax
from jax.experimental import pallas as pl
from jax.experimental.pallas import tpu as pltpu

HEAD_DIM = 128
MOBA_BLOCK = 256
MOBA_TOPK = 3
ROPE_THETA = 10000.0
N_GROUPS = 4
TOP_K = 2
LN_EPS = 1e-5
NEG_INF = -1e30
LOG2E = math.log2(math.e)
Q_SCALE = HEAD_DIM ** -0.5 * LOG2E
MOE_ROWS = 256
CAST_ROWS = 256
FF_CHUNK = 256
LANES = 128
VMEM_LIMIT = 56 * 1024 * 1024
ATTN_TILE = 2048
ATTN_KEY_TILE = 1024
ATTN_SUB = 256

F32 = jnp.float32
BF16 = jnp.bfloat16
NT_DIMS = (((1,), (1,)), ((), ()))


def _params(*sem):
    return pltpu.CompilerParams(dimension_semantics=sem, vmem_limit_bytes=VMEM_LIMIT)


def _split3(a):
    hi = a.astype(BF16)
    r1 = a - hi.astype(F32)
    mid = r1.astype(BF16)
    lo = (r1 - mid.astype(F32)).astype(BF16)
    return hi, mid, lo


def _in_proj_kernel(x_ref, w_ref, cos_ref, sin_ref, o_ref, kbar_ref, *, tm, tn, j0, j1, j2, j3):
    acc = lax.dot_general(x_ref[...], w_ref[0].astype(BF16), NT_DIMS, preferred_element_type=F32)
    j = pl.program_id(1)

    @pl.when(j < j0)
    def _():
        o_ref[...] = (acc * Q_SCALE).astype(BF16)

    @pl.when(((j >= j0) & (j < j1)) | (j >= j3))
    def _():
        o_ref[...] = acc.astype(BF16)

    def roped(c):
        ch = acc[:, c * HEAD_DIM:(c + 1) * HEAD_DIM]
        return ch * cos_ref[...] + pltpu.roll(ch, HEAD_DIM // 2, 1) * sin_ref[...]

    @pl.when((j >= j1) & (j < j2))
    def _():
        for c in range(tn // HEAD_DIM):
            o_ref[:, c * HEAD_DIM:(c + 1) * HEAD_DIM] = (roped(c) * Q_SCALE).astype(BF16)

    @pl.when((j >= j2) & (j < j3))
    def _():
        for c in range(tn // HEAD_DIM):
            r = roped(c)
            o_ref[:, c * HEAD_DIM:(c + 1) * HEAD_DIM] = r.astype(BF16)
            kbar_ref[0, :, c * HEAD_DIM:(c + 1) * HEAD_DIM] = (
                r.reshape(tm // MOBA_BLOCK, MOBA_BLOCK, HEAD_DIM).sum(axis=1) * (1.0 / MOBA_BLOCK))


def _in_proj(xb, w_in_t, layer, cos, sin, *, fox_w, moba_w, seq):
    n, d = xb.shape
    tm = min(1024, seq)
    tn = min(512, fox_w, moba_w)
    fq, mq = fox_w // tn, moba_w // tn
    j0, j1 = fq, 3 * fq
    j2, j3 = j1 + mq, j1 + 2 * mq
    nj = j1 + 3 * mq
    spb = seq // tm
    kern = functools.partial(_in_proj_kernel, tm=tm, tn=tn, j0=j0, j1=j1, j2=j2, j3=j3)
    return pl.pallas_call(
        kern,
        grid=(n // tm, nj),
        in_specs=[
            pl.BlockSpec((tm, d), lambda i, j: (i, 0)),
            pl.BlockSpec((1, tn, d), lambda i, j: (layer, j, 0)),
            pl.BlockSpec((tm, HEAD_DIM), lambda i, j: (i % spb, 0)),
            pl.BlockSpec((tm, HEAD_DIM), lambda i, j: (i % spb, 0)),
        ],
        out_specs=[
            pl.BlockSpec((tm, tn), lambda i, j: (i, j)),
            pl.BlockSpec((1, tm // MOBA_BLOCK, tn),
                         lambda i, j: (i, 0, jnp.clip(j - j2, 0, mq - 1))),
        ],
        out_shape=[
            jax.ShapeDtypeStruct((n, 3 * (fox_w + moba_w)), BF16),
            jax.ShapeDtypeStruct((n // tm, tm // MOBA_BLOCK, moba_w), F32),
        ],
        compiler_params=_params("arbitrary", "arbitrary"),
        name="in_proj",
    )(xb, w_in_t, cos, sin)


def _forget_kernel(x_ref, w_ref, b_ref, cp_ref, carry_ref, *, ts, heads):
    @pl.when(pl.program_id(1) == 0)
    def _():
        carry_ref[...] = jnp.zeros_like(carry_ref)

    f = lax.dot_general(x_ref[...], w_ref[...].astype(BF16), NT_DIMS, preferred_element_type=F32) + b_ref[...]
    lf = jnp.minimum(f, 0.0) - jnp.log1p(jnp.exp(-jnp.abs(f)))
    row = lax.broadcasted_iota(jnp.int32, (ts, ts), 0)
    col = lax.broadcasted_iota(jnp.int32, (ts, ts), 1)
    tri = jnp.where(row >= col, 1.0, 0.0).astype(BF16)
    cs = carry_ref[...]
    for part in _split3(lf):
        cs = cs + jnp.dot(tri, part, preferred_element_type=F32)
    carry_ref[...] = cs[ts - 1:ts, :]

    lane = lax.broadcasted_iota(jnp.int32, (ts, LANES), 1)
    neg = jnp.where(lane < heads, -LOG2E * cs, 0.0)
    erow = lax.broadcasted_iota(jnp.int32, (LANES, LANES), 0)
    ecol = lax.broadcasted_iota(jnp.int32, (LANES, LANES), 1)
    out = jnp.zeros((ts, LANES), F32)
    for j, part in enumerate(_split3(neg)):
        spread = jnp.where(ecol == 3 * erow + j, 1.0, 0.0).astype(BF16)
        out = out + jnp.dot(part, spread, preferred_element_type=F32)
    cp_ref[...] = out.astype(BF16)


def _forget_cumsum(xb, w_f, b_f, *, batch, seq, heads):
    n, d = xb.shape
    assert 3 * heads <= LANES
    ts = min(512, seq)
    spb = seq // ts
    return pl.pallas_call(
        functools.partial(_forget_kernel, ts=ts, heads=heads),
        grid=(batch, spb),
        in_specs=[
            pl.BlockSpec((ts, d), lambda b, s: (b * spb + s, 0)),
            pl.BlockSpec((LANES, d), lambda b, s: (0, 0)),
            pl.BlockSpec((1, LANES), lambda b, s: (0, 0)),
        ],
        out_specs=pl.BlockSpec((ts, LANES), lambda b, s: (b * spb + s, 0)),
        out_shape=jax.ShapeDtypeStruct((n, LANES), BF16),
        scratch_shapes=[pltpu.VMEM((1, LANES), F32)],
        compiler_params=_params("arbitrary", "arbitrary"),
        name="forget_cumsum",
    )(xb, w_f, b_f)


def _softmax_update(m, l, s_t):
    m_new = jnp.maximum(m, s_t.max(axis=0, keepdims=True))
    alpha = jnp.exp2(m - m_new)
    p_t = jnp.exp2(s_t - m_new)
    return m_new, alpha * l + p_t.sum(axis=0, keepdims=True), alpha, p_t.astype(BF16)


LEAD = 4


def _attend_tile(q_subs, keys, values_t, masks, carry):
    n_sub = len(q_subs)

    def scores(r):
        s_t = jnp.dot(keys[r], q_subs[r], preferred_element_type=F32)
        return s_t if masks[r] is None else jnp.where(masks[r], s_t, NEG_INF)

    s = {r: scores(r) for r in range(min(LEAD, n_sub))}
    out = []
    for r in range(n_sub):
        m, l, acc_t = carry[r]
        m, l, alpha, p_t = _softmax_update(m, l, s.pop(r))
        if r + LEAD < n_sub:
            s[r + LEAD] = scores(r + LEAD)
        acc_t = alpha * acc_t + jnp.dot(values_t[r], p_t, preferred_element_type=F32)
        out.append((m, l, acc_t))
    return tuple(out)


def _transpose_bf16(a):
    return a.astype(F32).T.astype(BF16)


def _store_keys_values(k_ref, v_ref, kaug_ref, vt_ref, aug, *, chunk):
    seq = k_ref.shape[0]
    kaug_ref[:, :HEAD_DIM] = k_ref[...]
    kaug_ref[:, HEAD_DIM:] = aug
    for c in range(seq // chunk):
        vt_ref[:, c * chunk:(c + 1) * chunk] = _transpose_bf16(v_ref[c * chunk:(c + 1) * chunk, :])


def _flash_attention(qt_aug, kaug_ref, vt_ref, o_ref, qi, *, t, sub):
    n_sub = t // sub
    q_subs = [qt_aug[:, r * sub:(r + 1) * sub] for r in range(n_sub)]

    tk = min(ATTN_KEY_TILE, t)

    def past(ki, carry):
        start = pl.multiple_of(ki * tk, tk)
        k = kaug_ref[pl.ds(start, tk), :]
        v_t = vt_ref[:, pl.ds(start, tk)]
        return _attend_tile(q_subs, [k] * n_sub, [v_t] * n_sub, [None] * n_sub, carry)

    init = tuple((jnp.full((1, sub), -jnp.inf, F32), jnp.zeros((1, sub), F32), jnp.zeros((HEAD_DIM, sub), F32))
                 for _ in range(n_sub))
    carry = lax.fori_loop(0, qi * (t // tk), past, init)

    start = pl.multiple_of(qi * t, t)
    keys, values_t, masks = [], [], []
    for r in range(n_sub):
        width = (r + 1) * sub
        keys.append(kaug_ref[pl.ds(start, width), :])
        values_t.append(vt_ref[:, pl.ds(start, width)])
        key = lax.broadcasted_iota(jnp.int32, (width, sub), 0)
        qry = lax.broadcasted_iota(jnp.int32, (width, sub), 1) + r * sub
        masks.append(key <= qry)
    carry = _attend_tile(q_subs, keys, values_t, masks, carry)
    for r in range(n_sub):
        _, l, acc_t = carry[r]
        o_ref[r * sub:(r + 1) * sub, :] = (acc_t / l).T.astype(BF16)


def _fox_kernel(q_ref, k_ref, v_ref, cp_ref, o_ref, kaug_ref, vt_ref, *, t, sub):
    h = pl.program_id(1)
    qi = pl.program_id(2)

    @pl.when(qi == 0)
    def _():
        lane = lax.broadcasted_iota(jnp.int32, cp_ref.shape, 1)
        mine = (lane >= 3 * h) & (lane < 3 * h + 3)
        _store_keys_values(k_ref, v_ref, kaug_ref, vt_ref,
                           jnp.where(mine, cp_ref[...], jnp.zeros_like(cp_ref)), chunk=t)

    row = lax.broadcasted_iota(jnp.int32, (LANES, t), 0)
    ones_t = jnp.where((row >= 3 * h) & (row < 3 * h + 3), 1.0, 0.0).astype(BF16)
    qt_aug = jnp.concatenate([_transpose_bf16(q_ref[...]), ones_t], axis=0)
    _flash_attention(qt_aug, kaug_ref, vt_ref, o_ref, qi, t=t, sub=sub)


def _fox_attention(proj, cp, *, batch, seq, heads, k_col, v_col):
    n = proj.shape[0]
    t = min(ATTN_TILE, seq)
    sub = min(ATTN_SUB, t)
    nq = seq // t
    return pl.pallas_call(
        functools.partial(_fox_kernel, t=t, sub=sub),
        grid=(batch, heads, nq),
        in_specs=[
            pl.BlockSpec((t, HEAD_DIM), lambda b, h, qi: (b * nq + qi, h)),
            pl.BlockSpec((seq, HEAD_DIM), lambda b, h, qi: (b, k_col + h)),
            pl.BlockSpec((seq, HEAD_DIM), lambda b, h, qi: (b, v_col + h)),
            pl.BlockSpec((seq, LANES), lambda b, h, qi: (b, 0)),
        ],
        out_specs=pl.BlockSpec((t, HEAD_DIM), lambda b, h, qi: (b * nq + qi, h)),
        out_shape=jax.ShapeDtypeStruct((n, heads * HEAD_DIM), BF16),
        scratch_shapes=[pltpu.VMEM((seq, 2 * HEAD_DIM), BF16), pltpu.VMEM((HEAD_DIM, seq), BF16)],
        compiler_params=_params("arbitrary", "arbitrary", "arbitrary"),
        name="fox_attention",
    )(proj, proj, proj, cp)


def _moba_kernel(q_ref, k_ref, v_ref, kbar_ref, o_ref, kaug_ref, vt_ref, *, t, sub, nb):
    qi = pl.program_id(2)
    seq = k_ref.shape[0]

    @pl.when(qi == 0)
    def _():
        r = lax.broadcasted_iota(jnp.int32, (seq, LANES), 0)
        c = lax.broadcasted_iota(jnp.int32, (seq, LANES), 1)
        _store_keys_values(k_ref, v_ref, kaug_ref, vt_ref,
                           jnp.where(r // MOBA_BLOCK == c, 1.0, 0.0).astype(BF16), chunk=t)

    q = q_ref[...]
    gate = jnp.zeros((nb, t), F32)
    for part in _split3(kbar_ref[0]):
        gate = gate + lax.dot_general(part, q, NT_DIMS, preferred_element_type=F32)
    blk = lax.broadcasted_iota(jnp.int32, (nb, t), 0)
    own = qi * (t // MOBA_BLOCK) + lax.broadcasted_iota(jnp.int32, (nb, t), 1) // MOBA_BLOCK
    valid = blk < own
    g = jnp.where(valid, gate, NEG_INF)
    rank = jnp.zeros((nb, t), jnp.int32)
    for m in range(nb):
        gm = g[m:m + 1, :]
        beats = (gm > g) | ((gm == g) & (blk > m))
        rank = rank + jnp.where(beats, 1, 0)
    allowed = (valid & (rank < MOBA_TOPK)) | (blk == own)
    bias_t = jnp.where(allowed, 0.0, NEG_INF).astype(BF16)
    qt_aug = jnp.concatenate([_transpose_bf16(q), bias_t, jnp.zeros((LANES - nb, t), BF16)], axis=0)
    _flash_attention(qt_aug, kaug_ref, vt_ref, o_ref, qi, t=t, sub=sub)


def _moba_attention(proj, kbar, *, batch, seq, heads, q_col, k_col, v_col):
    n = proj.shape[0]
    nb = seq // MOBA_BLOCK
    assert nb <= LANES
    t = min(ATTN_TILE, seq)
    nq = seq // t
    return pl.pallas_call(
        functools.partial(_moba_kernel, t=t, sub=min(ATTN_SUB, t), nb=nb),
        grid=(batch, heads, nq),
        in_specs=[
            pl.BlockSpec((t, HEAD_DIM), lambda b, h, qi: (b * nq + qi, q_col + h)),
            pl.BlockSpec((seq, HEAD_DIM), lambda b, h, qi: (b, k_col + h)),
            pl.BlockSpec((seq, HEAD_DIM), lambda b, h, qi: (b, v_col + h)),
            pl.BlockSpec((1, nb, HEAD_DIM), lambda b, h, qi: (b, 0, h)),
        ],
        out_specs=pl.BlockSpec((t, HEAD_DIM), lambda b, h, qi: (b * nq + qi, h)),
        out_shape=jax.ShapeDtypeStruct((n, heads * HEAD_DIM), BF16),
        scratch_shapes=[pltpu.VMEM((seq, 2 * HEAD_DIM), BF16), pltpu.VMEM((HEAD_DIM, seq), BF16)],
        compiler_params=_params("arbitrary", "arbitrary", "arbitrary"),
        name="moba_attention",
    )(proj, proj, proj, kbar)


def _layer_norm_rows(z, g, b):
    mu = jnp.mean(z, axis=-1, keepdims=True)
    zc = z - mu
    var = jnp.mean(zc * zc, axis=-1, keepdims=True)
    return zc * lax.rsqrt(var + LN_EPS) * g + b


def _pack_bf16_pairs(x):
    half = x.shape[1] // 2
    lo = pltpu.bitcast(x[:, :half].astype(BF16).astype(F32), jnp.uint32)
    hi = pltpu.bitcast(x[:, half:].astype(BF16).astype(F32), jnp.uint32)
    return (lo >> 16) | (hi & jnp.uint32(0xFFFF0000))


def _unpack_bf16_pairs(p):
    lo = pltpu.bitcast(p << 16, F32).astype(BF16)
    hi = pltpu.bitcast(p & jnp.uint32(0xFFFF0000), F32).astype(BF16)
    return lo, hi


def _out_proj_kernel(of_ref, om_ref, w_ref, x_ref, g_ref, b_ref, o_ref, op_ref, *, tn, fox_w, alpha):
    j = pl.program_id(1)
    acc = jnp.dot(of_ref[...], w_ref[0, :fox_w, :], preferred_element_type=F32)
    acc = acc + jnp.dot(om_ref[...], w_ref[0, fox_w:, :], preferred_element_type=F32)
    o_ref[:, pl.ds(pl.multiple_of(j * tn, tn), tn)] = alpha * x_ref[...] + acc

    @pl.when(j == pl.num_programs(1) - 1)
    def _():
        out = _layer_norm_rows(o_ref[...], g_ref[...], b_ref[...])
        o_ref[...] = out
        op_ref[...] = _pack_bf16_pairs(out)


def _out_proj_ln(o_f, o_m, w, layer, x, g, b, *, alpha):
    n, d = x.shape
    fox_w, moba_w = o_f.shape[1], o_m.shape[1]
    tm = min(512, n)
    tn = min(512, d)
    kern = functools.partial(_out_proj_kernel, tn=tn, fox_w=fox_w, alpha=alpha)
    return pl.pallas_call(
        kern,
        grid=(n // tm, d // tn),
        in_specs=[
            pl.BlockSpec((tm, fox_w), lambda i, j: (i, 0)),
            pl.BlockSpec((tm, moba_w), lambda i, j: (i, 0)),
            pl.BlockSpec((1, fox_w + moba_w, tn), lambda i, j: (layer, 0, j)),
            pl.BlockSpec((tm, tn), lambda i, j: (i, j)),
            pl.BlockSpec((1, d), lambda i, j: (0, 0)),
            pl.BlockSpec((1, d), lambda i, j: (0, 0)),
        ],
        out_specs=[pl.BlockSpec((tm, d), lambda i, j: (i, 0)),
                   pl.BlockSpec((tm, d // 2), lambda i, j: (i, 0))],
        out_shape=[jax.ShapeDtypeStruct((n, d), F32), jax.ShapeDtypeStruct((n, d // 2), jnp.uint32)],
        compiler_params=_params("arbitrary", "arbitrary"),
        name="out_proj_ln",
    )(o_f, o_m, w, x, g, b)


def _router_kernel(x_ref, wt_ref, bias_ref, idx_ref, wts_ref, *, n_exp):
    x = x_ref[...]
    x_hi = x.astype(BF16)
    x_lo = (x - x_hi.astype(F32)).astype(BF16)
    w = wt_ref[...]
    w_hi = w.astype(BF16)
    w_lo = (w - w_hi.astype(F32)).astype(BF16)
    part = lax.dot_general(jnp.concatenate([w_hi, w_lo], axis=0), x_hi, NT_DIMS, preferred_element_type=F32)
    logits = (part[:n_exp] + part[n_exp:]
              + lax.dot_general(w_hi, x_lo, NT_DIMS, preferred_element_type=F32))
    ex = jnp.exp(logits - logits.max(axis=0, keepdims=True))
    probs = ex / ex.sum(axis=0, keepdims=True)
    sel = probs + bias_ref[...]
    per = n_exp // N_GROUPS
    gscore = []
    for gidx in range(N_GROUPS):
        r = [sel[gidx * per + a:gidx * per + a + 1, :] for a in range(per)]
        best = None
        for a in range(per):
            for b in range(a + 1, per):
                pair = r[a] + r[b]
                best = pair if best is None else jnp.maximum(best, pair)
        gscore.append(best)
    gmax = functools.reduce(jnp.maximum, gscore)
    g_idx = jnp.full_like(gmax, N_GROUPS - 1).astype(jnp.int32)
    for gidx in range(N_GROUPS - 2, -1, -1):
        g_idx = jnp.where(gscore[gidx] == gmax, gidx, g_idx)
    erow = lax.broadcasted_iota(jnp.int32, sel.shape, 0)
    masked = jnp.where(erow // per == g_idx, sel, NEG_INF)
    v1 = masked.max(axis=0, keepdims=True)
    i1 = jnp.min(jnp.where(masked == v1, erow, n_exp), axis=0, keepdims=True)
    masked2 = jnp.where(erow == i1, -jnp.inf, masked)
    v2 = masked2.max(axis=0, keepdims=True)
    i2 = jnp.min(jnp.where(masked2 == v2, erow, n_exp), axis=0, keepdims=True)
    w1 = jnp.sum(jnp.where(erow == i1, probs, 0.0), axis=0, keepdims=True)
    w2 = jnp.sum(jnp.where(erow == i2, probs, 0.0), axis=0, keepdims=True)
    tot = w1 + w2
    idx_ref[...] = jnp.concatenate([i1, i2], axis=0)
    wts_ref[...] = jnp.concatenate([w1 / tot, w2 / tot], axis=0)


def _router(x, w_router_t, bias_col):
    n, d = x.shape
    n_exp = w_router_t.shape[0]
    tm = min(512, n)
    return pl.pallas_call(
        functools.partial(_router_kernel, n_exp=n_exp),
        grid=(n // tm,),
        in_specs=[
            pl.BlockSpec((tm, d), lambda i: (i, 0)),
            pl.BlockSpec((n_exp, d), lambda i: (0, 0)),
            pl.BlockSpec((n_exp, 1), lambda i: (0, 0)),
        ],
        out_specs=[
            pl.BlockSpec((TOP_K, tm), lambda i: (0, i)),
            pl.BlockSpec((TOP_K, tm), lambda i: (0, i)),
        ],
        out_shape=[
            jax.ShapeDtypeStruct((TOP_K, n), jnp.int32),
            jax.ShapeDtypeStruct((TOP_K, n), F32),
        ],
        compiler_params=_params("arbitrary"),
        name="router",
    )(x, w_router_t, bias_col)


def _dispatch(idx, n_exp):
    k, n = idx.shape
    e = idx.reshape(-1)
    onehot = (e[:, None] == jnp.arange(n_exp, dtype=jnp.int32)[None, :]).astype(jnp.int32)
    cums = jnp.cumsum(onehot, axis=0)
    rank = jnp.take_along_axis(cums, e[:, None], axis=1)[:, 0] - 1
    counts = cums[-1]
    padded = (counts + MOE_ROWS - 1) // MOE_ROWS * MOE_ROWS
    pend = jnp.cumsum(padded)
    dest = (pend - padded)[e] + rank
    n_blocks = (k * n) // MOE_ROWS + n_exp
    tok = jnp.tile(jnp.arange(n, dtype=jnp.int32), k)
    row_tok = jnp.zeros((n_blocks * MOE_ROWS,), jnp.int32).at[dest].set(tok)
    blk_start = jnp.arange(n_blocks, dtype=jnp.int32) * MOE_ROWS
    blk_expert = jnp.minimum(
        jnp.sum((pend[None, :] <= blk_start[:, None]).astype(jnp.int32), axis=1), n_exp - 1)
    n_used = (pend[-1:] // MOE_ROWS).astype(jnp.int32)
    used = jnp.arange(n_blocks, dtype=jnp.int32) < n_used[0]
    later = used[None, :] & (blk_expert[None, :] > blk_expert[:, None])
    next_expert = jnp.min(jnp.where(later, blk_expert[None, :], n_exp), axis=1)
    next_expert = jnp.where(next_expert >= n_exp, -1, next_expert).astype(jnp.int32)
    return dest.astype(jnp.int32), row_tok, blk_expert, next_expert, n_used


def _start_row_gather(idx_ref, base, count, src_hbm, dst_ref, sem):
    for i in range(count):
        r = idx_ref[base + i]
        pltpu.make_async_copy(src_hbm.at[pl.ds(r, 1), :], dst_ref.at[pl.ds(i, 1), :], sem).start()


def _wait_row_gather(count, src_hbm, dst_ref, sem):
    pltpu.make_async_copy(src_hbm.at[pl.ds(0, count), :], dst_ref, sem).wait()


def _start_row_gather_loop(idx_ref, base, count, src_hbm, dst_ref, sem):
    def issue(i, c):
        r = idx_ref[base + i]
        pltpu.make_async_copy(src_hbm.at[pl.ds(r, 1), :], dst_ref.at[pl.ds(i, 1), :], sem).start()
        return c
    lax.fori_loop(0, count, issue, 0, unroll=8)


def _moe_up_kernel(blk_ref, nxt_exp_ref, tok_ref, nused_ref, xp_hbm, wg_hbm, wu_hbm, h_ref,
                   xg_ref, wf_ref, wgb_ref, wub_ref, sems, wsems, *, layer):
    b = pl.program_id(0)
    n_used = nused_ref[0]

    def weight_copies(e):
        return (pltpu.make_async_copy(wg_hbm.at[layer, e], wf_ref.at[0], wsems.at[0]),
                pltpu.make_async_copy(wu_hbm.at[layer, e], wf_ref.at[1], wsems.at[1]))

    @pl.when((b == 0) & (n_used > 0))
    def _():
        for c in weight_copies(blk_ref[0]):
            c.start()
        _start_row_gather_loop(tok_ref, 0, MOE_ROWS, xp_hbm, xg_ref.at[0], sems.at[0])

    @pl.when((b < n_used) & ((b == 0) | (blk_ref[b] != blk_ref[jnp.maximum(b - 1, 0)])))
    def _():
        for c in weight_copies(blk_ref[b]):
            c.wait()
        def cast_rows(c, carry):
            sl = pl.ds(pl.multiple_of(c * CAST_ROWS, CAST_ROWS), CAST_ROWS)
            wgb_ref[sl, :] = wf_ref[0, sl, :].astype(BF16)
            wub_ref[sl, :] = wf_ref[1, sl, :].astype(BF16)
            return carry
        lax.fori_loop(0, wgb_ref.shape[0] // CAST_ROWS, cast_rows, 0)

        @pl.when(nxt_exp_ref[b] >= 0)
        def _():
            for c in weight_copies(nxt_exp_ref[b]):
                c.start()

    @pl.when(b < n_used)
    def _():
        slot = b % 2
        _wait_row_gather(MOE_ROWS, xp_hbm, xg_ref.at[slot], sems.at[slot])

        @pl.when(b + 1 < n_used)
        def _():
            _start_row_gather_loop(tok_ref, (b + 1) * MOE_ROWS, MOE_ROWS, xp_hbm,
                                   xg_ref.at[1 - slot], sems.at[1 - slot])

        xb = jnp.concatenate(_unpack_bf16_pairs(xg_ref[slot]), axis=1)
        chunk = min(FF_CHUNK, h_ref.shape[1])
        for c in range(h_ref.shape[1] // chunk):
            sl = slice(c * chunk, (c + 1) * chunk)
            g = jnp.dot(xb, wgb_ref[:, sl], preferred_element_type=F32)
            u = jnp.dot(xb, wub_ref[:, sl], preferred_element_type=F32)
            h_ref[:, sl] = (g * jax.nn.sigmoid(g) * u).astype(BF16)

    @pl.when(b >= n_used)
    def _():
        h_ref[...] = jnp.zeros_like(h_ref)


def _moe_up(xp, w_gate, w_up, layer, blk_expert, next_expert, row_tok, n_used):
    n, dp = xp.shape
    n_blocks = blk_expert.shape[0]
    d, d_ff = w_gate.shape[-2:]
    return pl.pallas_call(
        functools.partial(_moe_up_kernel, layer=layer),
        grid_spec=pltpu.PrefetchScalarGridSpec(
            num_scalar_prefetch=4,
            grid=(n_blocks,),
            in_specs=[
                pl.BlockSpec(memory_space=pl.ANY),
                pl.BlockSpec(memory_space=pl.ANY),
                pl.BlockSpec(memory_space=pl.ANY),
            ],
            out_specs=pl.BlockSpec((MOE_ROWS, d_ff), lambda b, blk, nxt, tok, nu: (b, 0)),
            scratch_shapes=[pltpu.VMEM((2, MOE_ROWS, dp), jnp.uint32),
                            pltpu.VMEM((2, d, d_ff), F32), pltpu.VMEM((d, d_ff), BF16), pltpu.VMEM((d, d_ff), BF16),
                            pltpu.SemaphoreType.DMA((2,)), pltpu.SemaphoreType.DMA((2,))],
        ),
        out_shape=jax.ShapeDtypeStruct((n_blocks * MOE_ROWS, d_ff), BF16),
        compiler_params=_params("arbitrary"),
        name="moe_up",
    )(blk_expert, next_expert, row_tok, n_used, xp, w_gate, w_up)


def _moe_down_kernel(blk_ref, nused_ref, h_ref, wd_ref, y_ref, wb_ref):
    b = pl.program_id(0)
    new_expert = (b == 0) | (blk_ref[b] != blk_ref[jnp.maximum(b - 1, 0)])

    @pl.when((b < nused_ref[0]) & new_expert)
    def _():
        wb_ref[...] = wd_ref[0, 0].astype(BF16)

    @pl.when(b < nused_ref[0])
    def _():
        y_ref[...] = jnp.dot(h_ref[...], wb_ref[...], preferred_element_type=F32)

    @pl.when(b >= nused_ref[0])
    def _():
        y_ref[...] = jnp.zeros_like(y_ref)


def _moe_down(h, w_down, layer, blk_expert, n_used):
    n_blocks = blk_expert.shape[0]
    d_ff, d = w_down.shape[2:]
    return pl.pallas_call(
        _moe_down_kernel,
        grid_spec=pltpu.PrefetchScalarGridSpec(
            num_scalar_prefetch=2,
            grid=(n_blocks,),
            in_specs=[
                pl.BlockSpec((MOE_ROWS, d_ff), lambda b, blk, nu: (b, 0)),
                pl.BlockSpec((1, 1, d_ff, d), lambda b, blk, nu: (layer, blk[b], 0, 0)),
            ],
            out_specs=pl.BlockSpec((MOE_ROWS, d), lambda b, blk, nu: (b, 0)),
            scratch_shapes=[pltpu.VMEM((d_ff, d), BF16)],
        ),
        out_shape=jax.ShapeDtypeStruct((n_blocks * MOE_ROWS, d), F32),
        compiler_params=_params("arbitrary"),
        name="moe_down",
    )(blk_expert, n_used, h, w_down)


def _combine_kernel(dest_ref, y_hbm, x_ref, wt_ref, g_ref, b_ref, o_ref, ob_ref,
                    ya0_ref, yb0_ref, ya1_ref, yb1_ref, sems, *, tc, n_tok, alpha):
    i = pl.program_id(0)
    last = pl.num_programs(0) - 1
    nxt_base = jnp.minimum(i + 1, last) * tc

    def start(base, ya_ref, yb_ref, slot):
        _start_row_gather(dest_ref, base, tc, y_hbm, ya_ref, sems.at[slot, 0])
        _start_row_gather(dest_ref, n_tok + base, tc, y_hbm, yb_ref, sems.at[slot, 1])

    def wait(ya_ref, yb_ref, slot):
        _wait_row_gather(tc, y_hbm, ya_ref, sems.at[slot, 0])
        _wait_row_gather(tc, y_hbm, yb_ref, sems.at[slot, 1])

    @pl.when(i == 0)
    def _():
        start(0, ya0_ref, yb0_ref, 0)

    def step(cur, nxt):
        wait(*cur)
        start(nxt_base, *nxt)
        moe = wt_ref[:, 0:1] * cur[0][...] + wt_ref[:, 1:2] * cur[1][...]
        out = _layer_norm_rows(alpha * x_ref[...] + moe, g_ref[...], b_ref[...])
        o_ref[...] = out
        ob_ref[...] = out.astype(BF16)

    slot0 = (ya0_ref, yb0_ref, 0)
    slot1 = (ya1_ref, yb1_ref, 1)

    @pl.when(i % 2 == 0)
    def _():
        step(slot0, slot1)

    @pl.when(i % 2 == 1)
    def _():
        step(slot1, slot0)

    @pl.when(i == last)
    def _():
        @pl.when(i % 2 == 0)
        def _():
            wait(*slot1)

        @pl.when(i % 2 == 1)
        def _():
            wait(*slot0)


def _combine_ln(y, x, wts_tok, dest, g, b, *, alpha):
    n, d = x.shape
    tc = min(256, n)
    kern = functools.partial(_combine_kernel, tc=tc, n_tok=n, alpha=alpha)
    return pl.pallas_call(
        kern,
        grid_spec=pltpu.PrefetchScalarGridSpec(
            num_scalar_prefetch=1,
            grid=(n // tc,),
            in_specs=[
                pl.BlockSpec(memory_space=pl.ANY),
                pl.BlockSpec((tc, d), lambda i, dest: (i, 0)),
                pl.BlockSpec((tc, TOP_K), lambda i, dest: (i, 0)),
                pl.BlockSpec((1, d), lambda i, dest: (0, 0)),
                pl.BlockSpec((1, d), lambda i, dest: (0, 0)),
            ],
            out_specs=[
                pl.BlockSpec((tc, d), lambda i, dest: (i, 0)),
                pl.BlockSpec((tc, d), lambda i, dest: (i, 0)),
            ],
            scratch_shapes=[pltpu.VMEM((tc, d), F32)] * 4 + [pltpu.SemaphoreType.DMA((2, 2))],
        ),
        out_shape=[jax.ShapeDtypeStruct((n, d), F32), jax.ShapeDtypeStruct((n, d), BF16)],
        compiler_params=_params("arbitrary"),
        name="combine_ln",
    )(dest, y, x, wts_tok, g, b)


def _rope_tables(seq):
    half = HEAD_DIM // 2
    inv = ROPE_THETA ** (-jnp.arange(half, dtype=F32) / half)
    ang = jnp.arange(seq, dtype=jnp.int32).astype(F32)[:, None] * inv[None, :]
    cos, sin = jnp.cos(ang), jnp.sin(ang)
    return jnp.concatenate([cos, cos], axis=-1), jnp.concatenate([-sin, sin], axis=-1)


def kernel(x, w_in, b_forget, w_out, ln1_g, ln1_b, w_router, router_bias, w_gate, w_up, w_down, ln2_g, ln2_b):
    batch, seq, d = x.shape
    depth = w_in.shape[0]
    fox_heads = b_forget.shape[-1]
    fox_w = fox_heads * HEAD_DIM
    moba_w = w_out.shape[1] - fox_w
    moba_heads = moba_w // HEAD_DIM
    n_exp = w_router.shape[1]
    n = batch * seq
    alpha = (2.0 * depth) ** 0.25
    qkv_w = 3 * (fox_w + moba_w)
    hb = HEAD_DIM

    cos, sin = _rope_tables(seq)
    w_router_t = w_router.T
    bias_col = router_bias.reshape(n_exp, 1)
    xf = x.reshape(n, d)
    xb = xf.astype(BF16)
    w_in_t = jnp.swapaxes(w_in, 1, 2)
    w_out_b = w_out.astype(BF16)

    for l in range(depth):
        w_f = jnp.pad(w_in_t[l, qkv_w:, :], ((0, LANES - fox_heads), (0, 0)))
        b_f = jnp.pad(b_forget[l], (0, LANES - fox_heads)).reshape(1, LANES)

        proj, kbar = _in_proj(xb, w_in_t, l, cos, sin, fox_w=fox_w, moba_w=moba_w, seq=seq)
        cp = _forget_cumsum(xb, w_f, b_f, batch=batch, seq=seq, heads=fox_heads)
        o_f = _fox_attention(proj, cp, batch=batch, seq=seq, heads=fox_heads,
                             k_col=fox_w // hb, v_col=2 * fox_w // hb)
        o_m = _moba_attention(proj, kbar.reshape(batch, seq // MOBA_BLOCK, moba_w),
                              batch=batch, seq=seq, heads=moba_heads,
                              q_col=3 * fox_w // hb, k_col=(3 * fox_w + moba_w) // hb,
                              v_col=(3 * fox_w + 2 * moba_w) // hb)
        x1, x1p = _out_proj_ln(o_f, o_m, w_out_b, l, xf,
                               ln1_g[l].reshape(1, d), ln1_b[l].reshape(1, d), alpha=alpha)

        idx, wts = _router(x1, w_router_t, bias_col)
        dest, row_tok, blk_expert, next_expert, n_used = _dispatch(idx, n_exp)
        h = _moe_up(x1p, w_gate, w_up, l, blk_expert, next_expert, row_tok, n_used)
        y = _moe_down(h, w_down, l, blk_expert, n_used)
        xf, xb = _combine_ln(y, x1, wts.T, dest, ln2_g[l].reshape(1, d), ln2_b[l].reshape(1, d),
                             alpha=alpha)

    return xf.reshape(batch, seq, d)
```

```python
import functools
import math

import jax
import jax.numpy as jnp
from jax import lax
from jax.experimental import pallas as pl
from jax.experimental.pallas import tpu as pltpu

HEAD_DIM = 128
MOBA_BLOCK = 256
MOBA_TOPK = 3
ROPE_THETA = 10000.0
N_GROUPS = 4
TOP_K = 2
LN_EPS = 1e-5
NEG_INF = -1e30
LOG2E = math.log2(math.e)
Q_SCALE = HEAD_DIM ** -0.5 * LOG2E
MOE_ROWS = 256
CAST_ROWS = 256
FF_CHUNK = 256
LANES = 128
VMEM_LIMIT = 56 * 1024 * 1024
ATTN_TILE = 2048
ATTN_KEY_TILE = 1024
ATTN_SUB = 256

F32 = jnp.float32
BF16 = jnp.bfloat16
NT_DIMS = (((1,), (1,)), ((), ()))


def _params(*sem):
    return pltpu.CompilerParams(dimension_semantics=sem, vmem_limit_bytes=VMEM_LIMIT)


def _split3(a):
    hi = a.astype(BF16)
    r1 = a - hi.astype(F32)
    mid = r1.astype(BF16)
    lo = (r1 - mid.astype(F32)).astype(BF16)
    return hi, mid, lo


def _in_proj_kernel(x_ref, w_ref, cos_ref, sin_ref, o_ref, kbar_ref, *, tm, tn, j0, j1, j2, j3):
    acc = lax.dot_general(x_ref[...], w_ref[0].astype(BF16), NT_DIMS, preferred_element_type=F32)
    j = pl.program_id(1)

    @pl.when(j < j0)
    def _():
        o_ref[...] = (acc * Q_SCALE).astype(BF16)

    @pl.when(((j >= j0) & (j < j1)) | (j >= j3))
    def _():
        o_ref[...] = acc.astype(BF16)

    def roped(c):
        ch = acc[:, c * HEAD_DIM:(c + 1) * HEAD_DIM]
        return ch * cos_ref[...] + pltpu.roll(ch, HEAD_DIM // 2, 1) * sin_ref[...]

    @pl.when((j >= j1) & (j < j2))
    def _():
        for c in range(tn // HEAD_DIM):
            o_ref[:, c * HEAD_DIM:(c + 1) * HEAD_DIM] = (roped(c) * Q_SCALE).astype(BF16)

    @pl.when((j >= j2) & (j < j3))
    def _():
        for c in range(tn // HEAD_DIM):
            r = roped(c)
            o_ref[:, c * HEAD_DIM:(c + 1) * HEAD_DIM] = r.astype(BF16)
            kbar_ref[0, :, c * HEAD_DIM:(c + 1) * HEAD_DIM] = (
                r.reshape(tm // MOBA_BLOCK, MOBA_BLOCK, HEAD_DIM).sum(axis=1) * (1.0 / MOBA_BLOCK))


def _in_proj(xb, w_in_t, layer, cos, sin, *, fox_w, moba_w, seq):
    n, d = xb.shape
    tm = min(1024, seq)
    tn = min(512, fox_w, moba_w)
    fq, mq = fox_w // tn, moba_w // tn
    j0, j1 = fq, 3 * fq
    j2, j3 = j1 + mq, j1 + 2 * mq
    nj = j1 + 3 * mq
    spb = seq // tm
    kern = functools.partial(_in_proj_kernel, tm=tm, tn=tn, j0=j0, j1=j1, j2=j2, j3=j3)
    return pl.pallas_call(
        kern,
        grid=(n // tm, nj),
        in_specs=[
            pl.BlockSpec((tm, d), lambda i, j: (i, 0)),
            pl.BlockSpec((1, tn, d), lambda i, j: (layer, j, 0)),
            pl.BlockSpec((tm, HEAD_DIM), lambda i, j: (i % spb, 0)),
            pl.BlockSpec((tm, HEAD_DIM), lambda i, j: (i % spb, 0)),
        ],
        out_specs=[
            pl.BlockSpec((tm, tn), lambda i, j: (i, j)),
            pl.BlockSpec((1, tm // MOBA_BLOCK, tn),
                         lambda i, j: (i, 0, jnp.clip(j - j2, 0, mq - 1))),
        ],
        out_shape=[
            jax.ShapeDtypeStruct((n, 3 * (fox_w + moba_w)), BF16),
            jax.ShapeDtypeStruct((n // tm, tm // MOBA_BLOCK, moba_w), F32),
        ],
        compiler_params=_params("arbitrary", "arbitrary"),
        name="in_proj",
    )(xb, w_in_t, cos, sin)


def _forget_kernel(x_ref, w_ref, b_ref, cp_ref, carry_ref, *, ts, heads):
    @pl.when(pl.program_id(1) == 0)
    def _():
        carry_ref[...] = jnp.zeros_like(carry_ref)

    f = lax.dot_general(x_ref[...], w_ref[...].astype(BF16), NT_DIMS, preferred_element_type=F32) + b_ref[...]
    lf = jnp.minimum(f, 0.0) - jnp.log1p(jnp.exp(-jnp.abs(f)))
    row = lax.broadcasted_iota(jnp.int32, (ts, ts), 0)
    col = lax.broadcasted_iota(jnp.int32, (ts, ts), 1)
    tri = jnp.where(row >= col, 1.0, 0.0).astype(BF16)
    cs = carry_ref[...]
    for part in _split3(lf):
        cs = cs + jnp.dot(tri, part, preferred_element_type=F32)
    carry_ref[...] = cs[ts - 1:ts, :]

    lane = lax.broadcasted_iota(jnp.int32, (ts, LANES), 1)
    neg = jnp.where(lane < heads, -LOG2E * cs, 0.0)
    erow = lax.broadcasted_iota(jnp.int32, (LANES, LANES), 0)
    ecol = lax.broadcasted_iota(jnp.int32, (LANES, LANES), 1)
    out = jnp.zeros((ts, LANES), F32)
    for j, part in enumerate(_split3(neg)):
        spread = jnp.where(ecol == 3 * erow + j, 1.0, 0.0).astype(BF16)
        out = out + jnp.dot(part, spread, preferred_element_type=F32)
    cp_ref[...] = out.astype(BF16)


def _forget_cumsum(xb, w_f, b_f, *, batch, seq, heads):
    n, d = xb.shape
    assert 3 * heads <= LANES
    ts = min(512, seq)
    spb = seq // ts
    return pl.pallas_call(
        functools.partial(_forget_kernel, ts=ts, heads=heads),
        grid=(batch, spb),
        in_specs=[
            pl.BlockSpec((ts, d), lambda b, s: (b * spb + s, 0)),
            pl.BlockSpec((LANES, d), lambda b, s: (0, 0)),
            pl.BlockSpec((1, LANES), lambda b, s: (0, 0)),
        ],
        out_specs=pl.BlockSpec((ts, LANES), lambda b, s: (b * spb + s, 0)),
        out_shape=jax.ShapeDtypeStruct((n, LANES), BF16),
        scratch_shapes=[pltpu.VMEM((1, LANES), F32)],
        compiler_params=_params("arbitrary", "arbitrary"),
        name="forget_cumsum",
    )(xb, w_f, b_f)


def _softmax_update(m, l, s_t):
    m_new = jnp.maximum(m, s_t.max(axis=0, keepdims=True))
    alpha = jnp.exp2(m - m_new)
    p_t = jnp.exp2(s_t - m_new)
    return m_new, alpha * l + p_t.sum(axis=0, keepdims=True), alpha, p_t.astype(BF16)


LEAD = 4


def _attend_tile(q_subs, keys, values_t, masks, carry):
    n_sub = len(q_subs)

    def scores(r):
        s_t = jnp.dot(keys[r], q_subs[r], preferred_element_type=F32)
        if masks[r] is None:
            return s_t
        rows = masks[r].shape[0]
        tail = jnp.where(masks[r], s_t[s_t.shape[0] - rows:], NEG_INF)
        return tail if rows == s_t.shape[0] else jnp.concatenate([s_t[:s_t.shape[0] - rows], tail], axis=0)

    s = {r: scores(r) for r in range(min(LEAD, n_sub))}
    out = []
    for r in range(n_sub):
        m, l, acc_t = carry[r]
        m, l, alpha, p_t = _softmax_update(m, l, s.pop(r))
        if r + LEAD < n_sub:
            s[r + LEAD] = scores(r + LEAD)
        acc_t = alpha * acc_t + jnp.dot(values_t[r], p_t, preferred_element_type=F32)
        out.append((m, l, acc_t))
    return tuple(out)


def _transpose_bf16(a):
    return a.astype(F32).T.astype(BF16)


def _store_keys_values(k_ref, v_ref, kaug_ref, vt_ref, aug, *, chunk):
    seq = k_ref.shape[0]
    kaug_ref[:, :HEAD_DIM] = k_ref[...]
    kaug_ref[:, HEAD_DIM:] = aug
    for c in range(seq // chunk):
        vt_ref[:, c * chunk:(c + 1) * chunk] = _transpose_bf16(v_ref[c * chunk:(c + 1) * chunk, :])


def _flash_attention(qt_aug, kaug_ref, vt_ref, o_ref, qi, *, t, sub):
    n_sub = t // sub
    q_subs = [qt_aug[:, r * sub:(r + 1) * sub] for r in range(n_sub)]

    tk = min(ATTN_KEY_TILE, t)

    def past(ki, carry):
        start = pl.multiple_of(ki * tk, tk)
        k = kaug_ref[pl.ds(start, tk), :]
        v_t = vt_ref[:, pl.ds(start, tk)]
        return _attend_tile(q_subs, [k] * n_sub, [v_t] * n_sub, [None] * n_sub, carry)

    init = tuple((jnp.full((1, sub), -jnp.inf, F32), jnp.zeros((1, sub), F32), jnp.zeros((HEAD_DIM, sub), F32))
                 for _ in range(n_sub))
    carry = lax.fori_loop(0, qi * (t // tk), past, init)

    start = pl.multiple_of(qi * t, t)
    keys, values_t = [], []
    for r in range(n_sub):
        width = (r + 1) * sub
        keys.append(kaug_ref[pl.ds(start, width), :])
        values_t.append(vt_ref[:, pl.ds(start, width)])
    causal = (lax.broadcasted_iota(jnp.int32, (sub, sub), 0) <= lax.broadcasted_iota(jnp.int32, (sub, sub), 1))
    masks = [causal] * n_sub
    carry = _attend_tile(q_subs, keys, values_t, masks, carry)
    for r in range(n_sub):
        _, l, acc_t = carry[r]
        o_ref[r * sub:(r + 1) * sub, :] = (acc_t / l).T.astype(BF16)


def _fox_kernel(q_ref, k_ref, v_ref, cp_ref, o_ref, kaug_ref, vt_ref, *, t, sub):
    h = pl.program_id(1)
    qi = pl.program_id(2)

    @pl.when(qi == 0)
    def _():
        lane = lax.broadcasted_iota(jnp.int32, cp_ref.shape, 1)
        mine = (lane >= 3 * h) & (lane < 3 * h + 3)
        _store_keys_values(k_ref, v_ref, kaug_ref, vt_ref,
                           jnp.where(mine, cp_ref[...], jnp.zeros_like(cp_ref)), chunk=t)

    row = lax.broadcasted_iota(jnp.int32, (LANES, t), 0)
    ones_t = jnp.where((row >= 3 * h) & (row < 3 * h + 3), 1.0, 0.0).astype(BF16)
    qt_aug = jnp.concatenate([_transpose_bf16(q_ref[...]), ones_t], axis=0)
    _flash_attention(qt_aug, kaug_ref, vt_ref, o_ref, qi, t=t, sub=sub)


def _fox_attention(proj, cp, *, batch, seq, heads, k_col, v_col):
    n = proj.shape[0]
    t = min(ATTN_TILE, seq)
    sub = min(ATTN_SUB, t)
    nq = seq // t
    return pl.pallas_call(
        functools.partial(_fox_kernel, t=t, sub=sub),
        grid=(batch, heads, nq),
        in_specs=[
            pl.BlockSpec((t, HEAD_DIM), lambda b, h, qi: (b * nq + qi, h)),
            pl.BlockSpec((seq, HEAD_DIM), lambda b, h, qi: (b, k_col + h)),
            pl.BlockSpec((seq, HEAD_DIM), lambda b, h, qi: (b, v_col + h)),
            pl.BlockSpec((seq, LANES), lambda b, h, qi: (b, 0)),
        ],
        out_specs=pl.BlockSpec((t, HEAD_DIM), lambda b, h, qi: (b * nq + qi, h)),
        out_shape=jax.ShapeDtypeStruct((n, heads * HEAD_DIM), BF16),
        scratch_shapes=[pltpu.VMEM((seq, 2 * HEAD_DIM), BF16), pltpu.VMEM((HEAD_DIM, seq), BF16)],
        compiler_params=_params("arbitrary", "arbitrary", "arbitrary"),
        name="fox_attention",
    )(proj, proj, proj, cp)


def _moba_kernel(q_ref, k_ref, v_ref, kbar_ref, o_ref, kaug_ref, vt_ref, *, t, sub, nb):
    qi = pl.program_id(2)
    seq = k_ref.shape[0]

    @pl.when(qi == 0)
    def _():
        r = lax.broadcasted_iota(jnp.int32, (seq, LANES), 0)
        c = lax.broadcasted_iota(jnp.int32, (seq, LANES), 1)
        _store_keys_values(k_ref, v_ref, kaug_ref, vt_ref,
                           jnp.where(r // MOBA_BLOCK == c, 1.0, 0.0).astype(BF16), chunk=t)

    q = q_ref[...]
    gate = jnp.zeros((nb, t), F32)
    for part in _split3(kbar_ref[0]):
        gate = gate + lax.dot_general(part, q, NT_DIMS, preferred_element_type=F32)
    blk = lax.broadcasted_iota(jnp.int32, (nb, t), 0)
    own = qi * (t // MOBA_BLOCK) + lax.broadcasted_iota(jnp.int32, (nb, t), 1) // MOBA_BLOCK
    valid = blk < own
    g = jnp.where(valid, gate, NEG_INF)
    rank = jnp.zeros((nb, t), jnp.int32)
    for m in range(nb):
        gm = g[m:m + 1, :]
        beats = (gm > g) | ((gm == g) & (blk > m))
        rank = rank + jnp.where(beats, 1, 0)
    allowed = (valid & (rank < MOBA_TOPK)) | (blk == own)
    bias_t = jnp.where(allowed, 0.0, NEG_INF).astype(BF16)
    qt_aug = jnp.concatenate([_transpose_bf16(q), bias_t, jnp.zeros((LANES - nb, t), BF16)], axis=0)
    _flash_attention(qt_aug, kaug_ref, vt_ref, o_ref, qi, t=t, sub=sub)


def _moba_attention(proj, kbar, *, batch, seq, heads, q_col, k_col, v_col):
    n = proj.shape[0]
    nb = seq // MOBA_BLOCK
    assert nb <= LANES
    t = min(ATTN_TILE, seq)
    nq = seq // t
    return pl.pallas_call(
        functools.partial(_moba_kernel, t=t, sub=min(ATTN_SUB, t), nb=nb),
        grid=(batch, heads, nq),
        in_specs=[
            pl.BlockSpec((t, HEAD_DIM), lambda b, h, qi: (b * nq + qi, q_col + h)),
            pl.BlockSpec((seq, HEAD_DIM), lambda b, h, qi: (b, k_col + h)),
            pl.BlockSpec((seq, HEAD_DIM), lambda b, h, qi: (b, v_col + h)),
            pl.BlockSpec((1, nb, HEAD_DIM), lambda b, h, qi: (b, 0, h)),
        ],
        out_specs=pl.BlockSpec((t, HEAD_DIM), lambda b, h, qi: (b * nq + qi, h)),
        out_shape=jax.ShapeDtypeStruct((n, heads * HEAD_DIM), BF16),
        scratch_shapes=[pltpu.VMEM((seq, 2 * HEAD_DIM), BF16), pltpu.VMEM((HEAD_DIM, seq), BF16)],
        compiler_params=_params("arbitrary", "arbitrary", "arbitrary"),
        name="moba_attention",
    )(proj, proj, proj, kbar)


def _layer_norm_rows(z, g, b):
    mu = jnp.mean(z, axis=-1, keepdims=True)
    zc = z - mu
    var = jnp.mean(zc * zc, axis=-1, keepdims=True)
    return zc * lax.rsqrt(var + LN_EPS) * g + b


def _pack_bf16_pairs(x):
    half = x.shape[1] // 2
    lo = pltpu.bitcast(x[:, :half].astype(BF16).astype(F32), jnp.uint32)
    hi = pltpu.bitcast(x[:, half:].astype(BF16).astype(F32), jnp.uint32)
    return (lo >> 16) | (hi & jnp.uint32(0xFFFF0000))


def _unpack_pairs_f32(p):
    return pltpu.bitcast(p << 16, F32), pltpu.bitcast(p & jnp.uint32(0xFFFF0000), F32)


def _unpack_bf16_pairs(p):
    lo, hi = _unpack_pairs_f32(p)
    return lo.astype(BF16), hi.astype(BF16)


def _out_proj_kernel(of_ref, om_ref, w_ref, x_ref, g_ref, b_ref, o_ref, op_ref, *, tn, fox_w, alpha):
    j = pl.program_id(1)
    acc = jnp.dot(of_ref[...], w_ref[0, :fox_w, :], preferred_element_type=F32)
    acc = acc + jnp.dot(om_ref[...], w_ref[0, fox_w:, :], preferred_element_type=F32)
    o_ref[:, pl.ds(pl.multiple_of(j * tn, tn), tn)] = alpha * x_ref[...] + acc

    @pl.when(j == pl.num_programs(1) - 1)
    def _():
        out = _layer_norm_rows(o_ref[...], g_ref[...], b_ref[...])
        o_ref[...] = out
        op_ref[...] = _pack_bf16_pairs(out)


def _out_proj_ln(o_f, o_m, w, layer, x, g, b, *, alpha):
    n, d = x.shape
    fox_w, moba_w = o_f.shape[1], o_m.shape[1]
    tm = min(512, n)
    tn = min(512, d)
    kern = functools.partial(_out_proj_kernel, tn=tn, fox_w=fox_w, alpha=alpha)
    return pl.pallas_call(
        kern,
        grid=(n // tm, d // tn),
        in_specs=[
            pl.BlockSpec((tm, fox_w), lambda i, j: (i, 0)),
            pl.BlockSpec((tm, moba_w), lambda i, j: (i, 0)),
            pl.BlockSpec((1, fox_w + moba_w, tn), lambda i, j: (layer, 0, j)),
            pl.BlockSpec((tm, tn), lambda i, j: (i, j)),
            pl.BlockSpec((1, d), lambda i, j: (0, 0)),
            pl.BlockSpec((1, d), lambda i, j: (0, 0)),
        ],
        out_specs=[pl.BlockSpec((tm, d), lambda i, j: (i, 0)),
                   pl.BlockSpec((tm, d // 2), lambda i, j: (i, 0))],
        out_shape=[jax.ShapeDtypeStruct((n, d), F32), jax.ShapeDtypeStruct((n, d // 2), jnp.uint32)],
        compiler_params=_params("arbitrary", "arbitrary"),
        name="out_proj_ln",
    )(o_f, o_m, w, x, g, b)


def _router_kernel(x_ref, wt_ref, bias_ref, idx_ref, wts_ref, *, n_exp):
    x = x_ref[...]
    x_hi = x.astype(BF16)
    x_lo = (x - x_hi.astype(F32)).astype(BF16)
    w = wt_ref[...]
    w_hi = w.astype(BF16)
    w_lo = (w - w_hi.astype(F32)).astype(BF16)
    part = lax.dot_general(jnp.concatenate([w_hi, w_lo], axis=0), x_hi, NT_DIMS, preferred_element_type=F32)
    logits = (part[:n_exp] + part[n_exp:]
              + lax.dot_general(w_hi, x_lo, NT_DIMS, preferred_element_type=F32))
    ex = jnp.exp(logits - logits.max(axis=0, keepdims=True))
    probs = ex / ex.sum(axis=0, keepdims=True)
    sel = probs + bias_ref[...]
    per = n_exp // N_GROUPS
    gscore = []
    for gidx in range(N_GROUPS):
        r = [sel[gidx * per + a:gidx * per + a + 1, :] for a in range(per)]
        best = None
        for a in range(per):
            for b in range(a + 1, per):
                pair = r[a] + r[b]
                best = pair if best is None else jnp.maximum(best, pair)
        gscore.append(best)
    gmax = functools.reduce(jnp.maximum, gscore)
    g_idx = jnp.full_like(gmax, N_GROUPS - 1).astype(jnp.int32)
    for gidx in range(N_GROUPS - 2, -1, -1):
        g_idx = jnp.where(gscore[gidx] == gmax, gidx, g_idx)
    erow = lax.broadcasted_iota(jnp.int32, sel.shape, 0)
    masked = jnp.where(erow // per == g_idx, sel, NEG_INF)
    v1 = masked.max(axis=0, keepdims=True)
    i1 = jnp.min(jnp.where(masked == v1, erow, n_exp), axis=0, keepdims=True)
    masked2 = jnp.where(erow == i1, -jnp.inf, masked)
    v2 = masked2.max(axis=0, keepdims=True)
    i2 = jnp.min(jnp.where(masked2 == v2, erow, n_exp), axis=0, keepdims=True)
    w1 = jnp.sum(jnp.where(erow == i1, probs, 0.0), axis=0, keepdims=True)
    w2 = jnp.sum(jnp.where(erow == i2, probs, 0.0), axis=0, keepdims=True)
    tot = w1 + w2
    idx_ref[...] = jnp.concatenate([i1, i2], axis=0)
    wts_ref[...] = jnp.concatenate([w1 / tot, w2 / tot], axis=0)


def _router(x, w_router_t, bias_col):
    n, d = x.shape
    n_exp = w_router_t.shape[0]
    tm = min(512, n)
    return pl.pallas_call(
        functools.partial(_router_kernel, n_exp=n_exp),
        grid=(n // tm,),
        in_specs=[
            pl.BlockSpec((tm, d), lambda i: (i, 0)),
            pl.BlockSpec((n_exp, d), lambda i: (0, 0)),
            pl.BlockSpec((n_exp, 1), lambda i: (0, 0)),
        ],
        out_specs=[
            pl.BlockSpec((TOP_K, tm), lambda i: (0, i)),
            pl.BlockSpec((TOP_K, tm), lambda i: (0, i)),
        ],
        out_shape=[
            jax.ShapeDtypeStruct((TOP_K, n), jnp.int32),
            jax.ShapeDtypeStruct((TOP_K, n), F32),
        ],
        compiler_params=_params("arbitrary"),
        name="router",
    )(x, w_router_t, bias_col)


def _dispatch(idx, n_exp):
    k, n = idx.shape
    e = idx.reshape(-1)
    onehot = (e[:, None] == jnp.arange(n_exp, dtype=jnp.int32)[None, :]).astype(jnp.int32)
    cums = jnp.cumsum(onehot, axis=0)
    rank = jnp.take_along_axis(cums, e[:, None], axis=1)[:, 0] - 1
    counts = cums[-1]
    padded = (counts + MOE_ROWS - 1) // MOE_ROWS * MOE_ROWS
    pend = jnp.cumsum(padded)
    dest = (pend - padded)[e] + rank
    n_blocks = (k * n) // MOE_ROWS + n_exp
    tok = jnp.tile(jnp.arange(n, dtype=jnp.int32), k)
    row_tok = jnp.zeros((n_blocks * MOE_ROWS,), jnp.int32).at[dest].set(tok)
    blk_start = jnp.arange(n_blocks, dtype=jnp.int32) * MOE_ROWS
    blk_expert = jnp.minimum(
        jnp.sum((pend[None, :] <= blk_start[:, None]).astype(jnp.int32), axis=1), n_exp - 1)
    n_used = (pend[-1:] // MOE_ROWS).astype(jnp.int32)
    used = jnp.arange(n_blocks, dtype=jnp.int32) < n_used[0]
    later = used[None, :] & (blk_expert[None, :] > blk_expert[:, None])
    next_expert = jnp.min(jnp.where(later, blk_expert[None, :], n_exp), axis=1)
    next_expert = jnp.where(next_expert >= n_exp, -1, next_expert).astype(jnp.int32)
    return dest.astype(jnp.int32), row_tok, blk_expert, next_expert, n_used


def _start_row_gather(idx_ref, base, count, src_hbm, dst_ref, sem):
    for i in range(count):
        r = idx_ref[base + i]
        pltpu.make_async_copy(src_hbm.at[pl.ds(r, 1), :], dst_ref.at[pl.ds(i, 1), :], sem).start()


def _wait_row_gather(count, src_hbm, dst_ref, sem):
    pltpu.make_async_copy(src_hbm.at[pl.ds(0, count), :], dst_ref, sem).wait()


def _start_row_gather_loop(idx_ref, base, count, src_hbm, dst_ref, sem):
    def issue(i, c):
        r = idx_ref[base + i]
        pltpu.make_async_copy(src_hbm.at[pl.ds(r, 1), :], dst_ref.at[pl.ds(i, 1), :], sem).start()
        return c
    lax.fori_loop(0, count, issue, 0, unroll=8)


def _moe_up_kernel(blk_ref, nxt_exp_ref, tok_ref, nused_ref, xp_hbm, wg_hbm, wu_hbm, h_ref,
                   xg_ref, wf_ref, wgb_ref, wub_ref, sems, wsems, *, layer):
    b = pl.program_id(0)
    n_used = nused_ref[0]
    last = n_used - 1

    def weight_copies(e):
        return (pltpu.make_async_copy(wg_hbm.at[layer, e], wf_ref.at[0], wsems.at[0]),
                pltpu.make_async_copy(wu_hbm.at[layer, e], wf_ref.at[1], wsems.at[1]))

    @pl.when((b == 0) & (n_used > 0))
    def _():
        for c in weight_copies(blk_ref[0]):
            c.start()
        _start_row_gather_loop(tok_ref, 0, MOE_ROWS, xp_hbm, xg_ref.at[0], sems.at[0])
        _start_row_gather_loop(tok_ref, jnp.minimum(1, last) * MOE_ROWS, MOE_ROWS, xp_hbm,
                               xg_ref.at[1], sems.at[1])

    @pl.when((b < n_used) & ((b == 0) | (blk_ref[b] != blk_ref[jnp.maximum(b - 1, 0)])))
    def _():
        for c in weight_copies(blk_ref[b]):
            c.wait()
        def cast_rows(c, carry):
            sl = pl.ds(pl.multiple_of(c * CAST_ROWS, CAST_ROWS), CAST_ROWS)
            wgb_ref[sl, :] = wf_ref[0, sl, :].astype(BF16)
            wub_ref[sl, :] = wf_ref[1, sl, :].astype(BF16)
            return carry
        lax.fori_loop(0, wgb_ref.shape[0] // CAST_ROWS, cast_rows, 0)

        @pl.when(nxt_exp_ref[b] >= 0)
        def _():
            for c in weight_copies(nxt_exp_ref[b]):
                c.start()

    @pl.when(b < n_used)
    def _():
        slot = b % 3
        ahead = (b + 2) % 3
        _wait_row_gather(MOE_ROWS, xp_hbm, xg_ref.at[slot], sems.at[slot])

        ahead_base = jnp.minimum(b + 2, last) * MOE_ROWS
        xb = jnp.concatenate(_unpack_bf16_pairs(xg_ref[slot]), axis=1)
        chunk = min(FF_CHUNK, h_ref.shape[1])
        n_chunks = h_ref.shape[1] // chunk
        per = -(-MOE_ROWS // n_chunks)
        for c in range(n_chunks):
            for i in range(c * per, min((c + 1) * per, MOE_ROWS)):
                r = tok_ref[ahead_base + i]
                pltpu.make_async_copy(xp_hbm.at[pl.ds(r, 1), :], xg_ref.at[ahead, pl.ds(i, 1), :],
                                      sems.at[ahead]).start()
            sl = slice(c * chunk, (c + 1) * chunk)
            g = jnp.dot(xb, wgb_ref[:, sl], preferred_element_type=F32)
            u = jnp.dot(xb, wub_ref[:, sl], preferred_element_type=F32)
            h_ref[:, sl] = (g * jax.nn.sigmoid(g) * u).astype(BF16)

        @pl.when(b == last)
        def _():
            other = (b + 1) % 3
            _wait_row_gather(MOE_ROWS, xp_hbm, xg_ref.at[other], sems.at[other])
            _wait_row_gather(MOE_ROWS, xp_hbm, xg_ref.at[ahead], sems.at[ahead])

    @pl.when(b >= n_used)
    def _():
        h_ref[...] = jnp.zeros_like(h_ref)


def _moe_up(xp, w_gate, w_up, layer, blk_expert, next_expert, row_tok, n_used):
    n, dp = xp.shape
    n_blocks = blk_expert.shape[0]
    d, d_ff = w_gate.shape[-2:]
    return pl.pallas_call(
        functools.partial(_moe_up_kernel, layer=layer),
        grid_spec=pltpu.PrefetchScalarGridSpec(
            num_scalar_prefetch=4,
            grid=(n_blocks,),
            in_specs=[
                pl.BlockSpec(memory_space=pl.ANY),
                pl.BlockSpec(memory_space=pl.ANY),
                pl.BlockSpec(memory_space=pl.ANY),
            ],
            out_specs=pl.BlockSpec((MOE_ROWS, d_ff), lambda b, blk, nxt, tok, nu: (b, 0)),
            scratch_shapes=[pltpu.VMEM((3, MOE_ROWS, dp), jnp.uint32),
                            pltpu.VMEM((2, d, d_ff), F32), pltpu.VMEM((d, d_ff), BF16), pltpu.VMEM((d, d_ff), BF16),
                            pltpu.SemaphoreType.DMA((3,)), pltpu.SemaphoreType.DMA((2,))],
        ),
        out_shape=jax.ShapeDtypeStruct((n_blocks * MOE_ROWS, d_ff), BF16),
        compiler_params=_params("arbitrary"),
        name="moe_up",
    )(blk_expert, next_expert, row_tok, n_used, xp, w_gate, w_up)


def _moe_down_kernel(blk_ref, nused_ref, h_ref, wd_ref, y_ref, wb_ref):
    b = pl.program_id(0)
    new_expert = (b == 0) | (blk_ref[b] != blk_ref[jnp.maximum(b - 1, 0)])

    @pl.when((b < nused_ref[0]) & new_expert)
    def _():
        wb_ref[...] = wd_ref[0, 0].astype(BF16)

    @pl.when(b < nused_ref[0])
    def _():
        y_ref[...] = _pack_bf16_pairs(jnp.dot(h_ref[...], wb_ref[...], preferred_element_type=F32))

    @pl.when(b >= nused_ref[0])
    def _():
        y_ref[...] = jnp.zeros_like(y_ref)


def _moe_down(h, w_down, layer, blk_expert, n_used):
    n_blocks = blk_expert.shape[0]
    d_ff, d = w_down.shape[2:]
    return pl.pallas_call(
        _moe_down_kernel,
        grid_spec=pltpu.PrefetchScalarGridSpec(
            num_scalar_prefetch=2,
            grid=(n_blocks,),
            in_specs=[
                pl.BlockSpec((MOE_ROWS, d_ff), lambda b, blk, nu: (b, 0)),
                pl.BlockSpec((1, 1, d_ff, d), lambda b, blk, nu: (layer, blk[b], 0, 0)),
            ],
            out_specs=pl.BlockSpec((MOE_ROWS, d // 2), lambda b, blk, nu: (b, 0)),
            scratch_shapes=[pltpu.VMEM((d_ff, d), BF16)],
        ),
        out_shape=jax.ShapeDtypeStruct((n_blocks * MOE_ROWS, d // 2), jnp.uint32),
        compiler_params=_params("arbitrary"),
        name="moe_down",
    )(blk_expert, n_used, h, w_down)


def _combine_kernel(dest_ref, y_hbm, x_ref, wt_ref, g_ref, b_ref, o_ref, ob_ref,
                    ya0_ref, yb0_ref, ya1_ref, yb1_ref, sems, *, tc, n_tok, alpha):
    i = pl.program_id(0)
    last = pl.num_programs(0) - 1
    nxt_base = jnp.minimum(i + 1, last) * tc

    def start(base, ya_ref, yb_ref, slot):
        _start_row_gather(dest_ref, base, tc, y_hbm, ya_ref, sems.at[slot, 0])
        _start_row_gather(dest_ref, n_tok + base, tc, y_hbm, yb_ref, sems.at[slot, 1])

    def wait(ya_ref, yb_ref, slot):
        _wait_row_gather(tc, y_hbm, ya_ref, sems.at[slot, 0])
        _wait_row_gather(tc, y_hbm, yb_ref, sems.at[slot, 1])

    @pl.when(i == 0)
    def _():
        start(0, ya0_ref, yb0_ref, 0)

    def step(cur, nxt):
        wait(*cur)
        start(nxt_base, *nxt)
        ya = jnp.concatenate(_unpack_pairs_f32(cur[0][...]), axis=1)
        yb = jnp.concatenate(_unpack_pairs_f32(cur[1][...]), axis=1)
        moe = wt_ref[:, 0:1] * ya + wt_ref[:, 1:2] * yb
        out = _layer_norm_rows(alpha * x_ref[...] + moe, g_ref[...], b_ref[...])
        o_ref[...] = out
        ob_ref[...] = out.astype(BF16)

    slot0 = (ya0_ref, yb0_ref, 0)
    slot1 = (ya1_ref, yb1_ref, 1)

    @pl.when(i % 2 == 0)
    def _():
        step(slot0, slot1)

    @pl.when(i % 2 == 1)
    def _():
        step(slot1, slot0)

    @pl.when(i == last)
    def _():
        @pl.when(i % 2 == 0)
        def _():
            wait(*slot1)

        @pl.when(i % 2 == 1)
        def _():
            wait(*slot0)


def _combine_ln(y, x, wts_tok, dest, g, b, *, alpha):
    n, d = x.shape
    tc = min(256, n)
    kern = functools.partial(_combine_kernel, tc=tc, n_tok=n, alpha=alpha)
    return pl.pallas_call(
        kern,
        grid_spec=pltpu.PrefetchScalarGridSpec(
            num_scalar_prefetch=1,
            grid=(n // tc,),
            in_specs=[
                pl.BlockSpec(memory_space=pl.ANY),
                pl.BlockSpec((tc, d), lambda i, dest: (i, 0)),
                pl.BlockSpec((tc, TOP_K), lambda i, dest: (i, 0)),
                pl.BlockSpec((1, d), lambda i, dest: (0, 0)),
                pl.BlockSpec((1, d), lambda i, dest: (0, 0)),
            ],
            out_specs=[
                pl.BlockSpec((tc, d), lambda i, dest: (i, 0)),
                pl.BlockSpec((tc, d), lambda i, dest: (i, 0)),
            ],
            scratch_shapes=[pltpu.VMEM((tc, d // 2), jnp.uint32)] * 4 + [pltpu.SemaphoreType.DMA((2, 2))],
        ),
        out_shape=[jax.ShapeDtypeStruct((n, d), F32), jax.ShapeDtypeStruct((n, d), BF16)],
        compiler_params=_params("arbitrary"),
        name="combine_ln",
    )(dest, y, x, wts_tok, g, b)


def _rope_tables(seq):
    half = HEAD_DIM // 2
    inv = ROPE_THETA ** (-jnp.arange(half, dtype=F32) / half)
    ang = jnp.arange(seq, dtype=jnp.int32).astype(F32)[:, None] * inv[None, :]
    cos, sin = jnp.cos(ang), jnp.sin(ang)
    return jnp.concatenate([cos, cos], axis=-1), jnp.concatenate([-sin, sin], axis=-1)


def kernel(x, w_in, b_forget, w_out, ln1_g, ln1_b, w_router, router_bias, w_gate, w_up, w_down, ln2_g, ln2_b):
    batch, seq, d = x.shape
    depth = w_in.shape[0]
    fox_heads = b_forget.shape[-1]
    fox_w = fox_heads * HEAD_DIM
    moba_w = w_out.shape[1] - fox_w
    moba_heads = moba_w // HEAD_DIM
    n_exp = w_router.shape[1]
    n = batch * seq
    alpha = (2.0 * depth) ** 0.25
    qkv_w = 3 * (fox_w + moba_w)
    hb = HEAD_DIM

    cos, sin = _rope_tables(seq)
    w_router_t = w_router.T
    bias_col = router_bias.reshape(n_exp, 1)
    xf = x.reshape(n, d)
    xb = xf.astype(BF16)
    w_in_t = jnp.swapaxes(w_in, 1, 2)
    w_out_b = w_out.astype(BF16)

    for l in range(depth):
        w_f = jnp.pad(w_in_t[l, qkv_w:, :], ((0, LANES - fox_heads), (0, 0)))
        b_f = jnp.pad(b_forget[l], (0, LANES - fox_heads)).reshape(1, LANES)

        proj, kbar = _in_proj(xb, w_in_t, l, cos, sin, fox_w=fox_w, moba_w=moba_w, seq=seq)
        cp = _forget_cumsum(xb, w_f, b_f, batch=batch, seq=seq, heads=fox_heads)
        o_f = _fox_attention(proj, cp, batch=batch, seq=seq, heads=fox_heads,
                             k_col=fox_w // hb, v_col=2 * fox_w // hb)
        o_m = _moba_attention(proj, kbar.reshape(batch, seq // MOBA_BLOCK, moba_w),
                              batch=batch, seq=seq, heads=moba_heads,
                              q_col=3 * fox_w // hb, k_col=(3 * fox_w + moba_w) // hb,
                              v_col=(3 * fox_w + 2 * moba_w) // hb)
        x1, x1p = _out_proj_ln(o_f, o_m, w_out_b, l, xf,
                               ln1_g[l].reshape(1, d), ln1_b[l].reshape(1, d), alpha=alpha)

        idx, wts = _router(x1, w_router_t, bias_col)
        dest, row_tok, blk_expert, next_expert, n_used = _dispatch(idx, n_exp)
        h = _moe_up(x1p, w_gate, w_up, l, blk_expert, next_expert, row_tok, n_used)
        y = _moe_down(h, w_down, l, blk_expert, n_used)
        xf, xb = _combine_ln(y, x1, wts.T, dest, ln2_g[l].reshape(1, d), ln2_b[l].reshape(1, d),
                             alpha=alpha)

    return xf.reshape(batch, seq, d)
```

```python
import functools
import math

import jax
import jax.numpy as jnp
from jax import lax
from jax.experimental import pallas as pl
from jax.experimental.pallas import tpu as pltpu

HEAD_DIM = 128
MOBA_BLOCK = 256
MOBA_TOPK = 3
ROPE_THETA = 10000.0
N_GROUPS = 4
TOP_K = 2
LN_EPS = 1e-5
NEG_INF = -1e30
LOG2E = math.log2(math.e)
Q_SCALE = HEAD_DIM ** -0.5 * LOG2E
MOE_ROWS = 256
CAST_ROWS = 256
FF_CHUNK = 256
LANES = 128
VMEM_LIMIT = 56 * 1024 * 1024
ATTN_TILE = 2048
ATTN_KEY_TILE = 1024
ATTN_SUB = 256

F32 = jnp.float32
BF16 = jnp.bfloat16
NT_DIMS = (((1,), (1,)), ((), ()))


def _params(*sem):
    return pltpu.CompilerParams(dimension_semantics=sem, vmem_limit_bytes=VMEM_LIMIT)


def _split3(a):
    hi = a.astype(BF16)
    r1 = a - hi.astype(F32)
    mid = r1.astype(BF16)
    lo = (r1 - mid.astype(F32)).astype(BF16)
    return hi, mid, lo


def _in_proj_kernel(x_ref, w_ref, cos_ref, sin_ref, o_ref, kbar_ref, *, tm, tn, j0, j1, j2, j3):
    acc = lax.dot_general(x_ref[...], w_ref[0].astype(BF16), NT_DIMS, preferred_element_type=F32)
    j = pl.program_id(1)

    @pl.when(j < j0)
    def _():
        o_ref[...] = (acc * Q_SCALE).astype(BF16)

    @pl.when(((j >= j0) & (j < j1)) | (j >= j3))
    def _():
        o_ref[...] = acc.astype(BF16)

    def roped(c):
        ch = acc[:, c * HEAD_DIM:(c + 1) * HEAD_DIM]
        return ch * cos_ref[...] + pltpu.roll(ch, HEAD_DIM // 2, 1) * sin_ref[...]

    @pl.when((j >= j1) & (j < j2))
    def _():
        for c in range(tn // HEAD_DIM):
            o_ref[:, c * HEAD_DIM:(c + 1) * HEAD_DIM] = (roped(c) * Q_SCALE).astype(BF16)

    @pl.when((j >= j2) & (j < j3))
    def _():
        for c in range(tn // HEAD_DIM):
            r = roped(c)
            o_ref[:, c * HEAD_DIM:(c + 1) * HEAD_DIM] = r.astype(BF16)
            kbar_ref[0, :, c * HEAD_DIM:(c + 1) * HEAD_DIM] = (
                r.reshape(tm // MOBA_BLOCK, MOBA_BLOCK, HEAD_DIM).sum(axis=1) * (1.0 / MOBA_BLOCK))


def _in_proj(xb, w_in_t, layer, cos, sin, *, fox_w, moba_w, seq):
    n, d = xb.shape
    tm = min(1024, seq)
    tn = min(512, fox_w, moba_w)
    fq, mq = fox_w // tn, moba_w // tn
    j0, j1 = fq, 3 * fq
    j2, j3 = j1 + mq, j1 + 2 * mq
    nj = j1 + 3 * mq
    spb = seq // tm
    kern = functools.partial(_in_proj_kernel, tm=tm, tn=tn, j0=j0, j1=j1, j2=j2, j3=j3)
    return pl.pallas_call(
        kern,
        grid=(n // tm, nj),
        in_specs=[
            pl.BlockSpec((tm, d), lambda i, j: (i, 0)),
            pl.BlockSpec((1, tn, d), lambda i, j: (layer, j, 0)),
            pl.BlockSpec((tm, HEAD_DIM), lambda i, j: (i % spb, 0)),
            pl.BlockSpec((tm, HEAD_DIM), lambda i, j: (i % spb, 0)),
        ],
        out_specs=[
            pl.BlockSpec((tm, tn), lambda i, j: (i, j)),
            pl.BlockSpec((1, tm // MOBA_BLOCK, tn),
                         lambda i, j: (i, 0, jnp.clip(j - j2, 0, mq - 1))),
        ],
        out_shape=[
            jax.ShapeDtypeStruct((n, 3 * (fox_w + moba_w)), BF16),
            jax.ShapeDtypeStruct((n // tm, tm // MOBA_BLOCK, moba_w), F32),
        ],
        compiler_params=_params("arbitrary", "arbitrary"),
        name="in_proj",
    )(xb, w_in_t, cos, sin)


def _forget_kernel(x_ref, w_ref, b_ref, cp_ref, carry_ref, *, ts, heads):
    @pl.when(pl.program_id(1) == 0)
    def _():
        carry_ref[...] = jnp.zeros_like(carry_ref)

    f = lax.dot_general(x_ref[...], w_ref[...].astype(BF16), NT_DIMS, preferred_element_type=F32) + b_ref[...]
    lf = jnp.minimum(f, 0.0) - jnp.log1p(jnp.exp(-jnp.abs(f)))
    row = lax.broadcasted_iota(jnp.int32, (ts, ts), 0)
    col = lax.broadcasted_iota(jnp.int32, (ts, ts), 1)
    tri = jnp.where(row >= col, 1.0, 0.0).astype(BF16)
    cs = carry_ref[...]
    for part in _split3(lf):
        cs = cs + jnp.dot(tri, part, preferred_element_type=F32)
    carry_ref[...] = cs[ts - 1:ts, :]

    lane = lax.broadcasted_iota(jnp.int32, (ts, LANES), 1)
    neg = jnp.where(lane < heads, -LOG2E * cs, 0.0)
    erow = lax.broadcasted_iota(jnp.int32, (LANES, LANES), 0)
    ecol = lax.broadcasted_iota(jnp.int32, (LANES, LANES), 1)
    out = jnp.zeros((ts, LANES), F32)
    for j, part in enumerate(_split3(neg)):
        spread = jnp.where(ecol == 3 * erow + j, 1.0, 0.0).astype(BF16)
        out = out + jnp.dot(part, spread, preferred_element_type=F32)
    cp_ref[...] = out.astype(BF16)


def _forget_cumsum(xb, w_f, b_f, *, batch, seq, heads):
    n, d = xb.shape
    assert 3 * heads <= LANES
    ts = min(512, seq)
    spb = seq // ts
    return pl.pallas_call(
        functools.partial(_forget_kernel, ts=ts, heads=heads),
        grid=(batch, spb),
        in_specs=[
            pl.BlockSpec((ts, d), lambda b, s: (b * spb + s, 0)),
            pl.BlockSpec((LANES, d), lambda b, s: (0, 0)),
            pl.BlockSpec((1, LANES), lambda b, s: (0, 0)),
        ],
        out_specs=pl.BlockSpec((ts, LANES), lambda b, s: (b * spb + s, 0)),
        out_shape=jax.ShapeDtypeStruct((n, LANES), BF16),
        scratch_shapes=[pltpu.VMEM((1, LANES), F32)],
        compiler_params=_params("arbitrary", "arbitrary"),
        name="forget_cumsum",
    )(xb, w_f, b_f)


def _softmax_update(m, l, s_t):
    m_new = jnp.maximum(m, s_t.max(axis=0, keepdims=True))
    alpha = jnp.exp2(m - m_new)
    p_t = jnp.exp2(s_t - m_new)
    return m_new, alpha * l + p_t.sum(axis=0, keepdims=True), alpha, p_t.astype(BF16)


LEAD = 4


def _attend_tile(q_subs, keys, values_t, masks, carry):
    n_sub = len(q_subs)

    def scores(r):
        s_t = jnp.dot(keys[r], q_subs[r], preferred_element_type=F32)
        if masks[r] is None:
            return s_t
        rows = masks[r].shape[0]
        tail = jnp.where(masks[r], s_t[s_t.shape[0] - rows:], NEG_INF)
        return tail if rows == s_t.shape[0] else jnp.concatenate([s_t[:s_t.shape[0] - rows], tail], axis=0)

    s = {r: scores(r) for r in range(min(LEAD, n_sub))}
    out = []
    for r in range(n_sub):
        m, l, acc_t = carry[r]
        m, l, alpha, p_t = _softmax_update(m, l, s.pop(r))
        if r + LEAD < n_sub:
            s[r + LEAD] = scores(r + LEAD)
        acc_t = alpha * acc_t + jnp.dot(values_t[r], p_t, preferred_element_type=F32)
        out.append((m, l, acc_t))
    return tuple(out)


def _transpose_bf16(a):
    return a.astype(F32).T.astype(BF16)


def _store_keys_values(k_ref, v_ref, kaug_ref, vt_ref, aug, *, chunk):
    seq = k_ref.shape[0]
    kaug_ref[:, :HEAD_DIM] = k_ref[...]
    kaug_ref[:, HEAD_DIM:] = aug
    for c in range(seq // chunk):
        vt_ref[:, c * chunk:(c + 1) * chunk] = _transpose_bf16(v_ref[c * chunk:(c + 1) * chunk, :])


def _flash_attention(qt_aug, kaug_ref, vt_ref, o_ref, qi, *, t, sub):
    n_sub = t // sub
    q_subs = [qt_aug[:, r * sub:(r + 1) * sub] for r in range(n_sub)]

    tk = min(ATTN_KEY_TILE, t)

    def past(ki, carry):
        start = pl.multiple_of(ki * tk, tk)
        k = kaug_ref[pl.ds(start, tk), :]
        v_t = vt_ref[:, pl.ds(start, tk)]
        return _attend_tile(q_subs, [k] * n_sub, [v_t] * n_sub, [None] * n_sub, carry)

    init = tuple((jnp.full((1, sub), -jnp.inf, F32), jnp.zeros((1, sub), F32), jnp.zeros((HEAD_DIM, sub), F32))
                 for _ in range(n_sub))
    carry = lax.fori_loop(0, qi * (t // tk), past, init)

    start = pl.multiple_of(qi * t, t)
    keys, values_t = [], []
    for r in range(n_sub):
        width = (r + 1) * sub
        keys.append(kaug_ref[pl.ds(start, width), :])
        values_t.append(vt_ref[:, pl.ds(start, width)])
    causal = (lax.broadcasted_iota(jnp.int32, (sub, sub), 0) <= lax.broadcasted_iota(jnp.int32, (sub, sub), 1))
    masks = [causal] * n_sub
    carry = _attend_tile(q_subs, keys, values_t, masks, carry)
    for r in range(n_sub):
        _, l, acc_t = carry[r]
        o_ref[r * sub:(r + 1) * sub, :] = (acc_t / l).T.astype(BF16)


def _fox_kernel(q_ref, k_ref, v_ref, cp_ref, o_ref, kaug_ref, vt_ref, *, t, sub):
    h = pl.program_id(1)
    qi = pl.program_id(2)

    @pl.when(qi == 0)
    def _():
        lane = lax.broadcasted_iota(jnp.int32, cp_ref.shape, 1)
        mine = (lane >= 3 * h) & (lane < 3 * h + 3)
        _store_keys_values(k_ref, v_ref, kaug_ref, vt_ref,
                           jnp.where(mine, cp_ref[...], jnp.zeros_like(cp_ref)), chunk=t)

    row = lax.broadcasted_iota(jnp.int32, (LANES, t), 0)
    ones_t = jnp.where((row >= 3 * h) & (row < 3 * h + 3), 1.0, 0.0).astype(BF16)
    qt_aug = jnp.concatenate([_transpose_bf16(q_ref[...]), ones_t], axis=0)
    _flash_attention(qt_aug, kaug_ref, vt_ref, o_ref, qi, t=t, sub=sub)


def _fox_attention(proj, cp, *, batch, seq, heads, k_col, v_col):
    n = proj.shape[0]
    t = min(ATTN_TILE, seq)
    sub = min(ATTN_SUB, t)
    nq = seq // t
    return pl.pallas_call(
        functools.partial(_fox_kernel, t=t, sub=sub),
        grid=(batch, heads, nq),
        in_specs=[
            pl.BlockSpec((t, HEAD_DIM), lambda b, h, qi: (b * nq + qi, h)),
            pl.BlockSpec((seq, HEAD_DIM), lambda b, h, qi: (b, k_col + h)),
            pl.BlockSpec((seq, HEAD_DIM), lambda b, h, qi: (b, v_col + h)),
            pl.BlockSpec((seq, LANES), lambda b, h, qi: (b, 0)),
        ],
        out_specs=pl.BlockSpec((t, HEAD_DIM), lambda b, h, qi: (b * nq + qi, h)),
        out_shape=jax.ShapeDtypeStruct((n, heads * HEAD_DIM), BF16),
        scratch_shapes=[pltpu.VMEM((seq, 2 * HEAD_DIM), BF16), pltpu.VMEM((HEAD_DIM, seq), BF16)],
        compiler_params=_params("arbitrary", "arbitrary", "arbitrary"),
        name="fox_attention",
    )(proj, proj, proj, cp)


def _moba_kernel(q_ref, k_ref, v_ref, kbar_ref, o_ref, kaug_ref, vt_ref, *, t, sub, nb):
    qi = pl.program_id(2)
    seq = k_ref.shape[0]

    @pl.when(qi == 0)
    def _():
        r = lax.broadcasted_iota(jnp.int32, (seq, LANES), 0)
        c = lax.broadcasted_iota(jnp.int32, (seq, LANES), 1)
        _store_keys_values(k_ref, v_ref, kaug_ref, vt_ref,
                           jnp.where(r // MOBA_BLOCK == c, 1.0, 0.0).astype(BF16), chunk=t)

    q = q_ref[...]
    gate = jnp.zeros((nb, t), F32)
    for part in _split3(kbar_ref[0]):
        gate = gate + lax.dot_general(part, q, NT_DIMS, preferred_element_type=F32)
    blk = lax.broadcasted_iota(jnp.int32, (nb, t), 0)
    own = qi * (t // MOBA_BLOCK) + lax.broadcasted_iota(jnp.int32, (nb, t), 1) // MOBA_BLOCK
    valid = blk < own
    g = jnp.where(valid, gate, NEG_INF)
    rank = jnp.zeros((nb, t), jnp.int32)
    for m in range(nb):
        gm = g[m:m + 1, :]
        beats = (gm > g) | ((gm == g) & (blk > m))
        rank = rank + jnp.where(beats, 1, 0)
    allowed = (valid & (rank < MOBA_TOPK)) | (blk == own)
    bias_t = jnp.where(allowed, 0.0, NEG_INF).astype(BF16)
    qt_aug = jnp.concatenate([_transpose_bf16(q), bias_t, jnp.zeros((LANES - nb, t), BF16)], axis=0)
    _flash_attention(qt_aug, kaug_ref, vt_ref, o_ref, qi, t=t, sub=sub)


def _moba_attention(proj, kbar, *, batch, seq, heads, q_col, k_col, v_col):
    n = proj.shape[0]
    nb = seq // MOBA_BLOCK
    assert nb <= LANES
    t = min(ATTN_TILE, seq)
    nq = seq // t
    return pl.pallas_call(
        functools.partial(_moba_kernel, t=t, sub=min(ATTN_SUB, t), nb=nb),
        grid=(batch, heads, nq),
        in_specs=[
            pl.BlockSpec((t, HEAD_DIM), lambda b, h, qi: (b * nq + qi, q_col + h)),
            pl.BlockSpec((seq, HEAD_DIM), lambda b, h, qi: (b, k_col + h)),
            pl.BlockSpec((seq, HEAD_DIM), lambda b, h, qi: (b, v_col + h)),
            pl.BlockSpec((1, nb, HEAD_DIM), lambda b, h, qi: (b, 0, h)),
        ],
        out_specs=pl.BlockSpec((t, HEAD_DIM), lambda b, h, qi: (b * nq + qi, h)),
        out_shape=jax.ShapeDtypeStruct((n, heads * HEAD_DIM), BF16),
        scratch_shapes=[pltpu.VMEM((seq, 2 * HEAD_DIM), BF16), pltpu.VMEM((HEAD_DIM, seq), BF16)],
        compiler_params=_params("arbitrary", "arbitrary", "arbitrary"),
        name="moba_attention",
    )(proj, proj, proj, kbar)


def _layer_norm_rows(z, g, b):
    mu = jnp.mean(z, axis=-1, keepdims=True)
    zc = z - mu
    var = jnp.mean(zc * zc, axis=-1, keepdims=True)
    return zc * lax.rsqrt(var + LN_EPS) * g + b


def _pack_bf16_pairs(x):
    half = x.shape[1] // 2
    lo = pltpu.bitcast(x[:, :half].astype(BF16).astype(F32), jnp.uint32)
    hi = pltpu.bitcast(x[:, half:].astype(BF16).astype(F32), jnp.uint32)
    return (lo >> 16) | (hi & jnp.uint32(0xFFFF0000))


def _unpack_pairs_f32(p):
    return pltpu.bitcast(p << 16, F32), pltpu.bitcast(p & jnp.uint32(0xFFFF0000), F32)


def _unpack_bf16_pairs(p):
    lo, hi = _unpack_pairs_f32(p)
    return lo.astype(BF16), hi.astype(BF16)


def _out_proj_kernel(of_ref, om_ref, w_ref, x_ref, g_ref, b_ref, o_ref, op_ref, *, tn, fox_w, alpha):
    j = pl.program_id(1)
    acc = jnp.dot(of_ref[...], w_ref[0, :fox_w, :], preferred_element_type=F32)
    acc = acc + jnp.dot(om_ref[...], w_ref[0, fox_w:, :], preferred_element_type=F32)
    o_ref[:, pl.ds(pl.multiple_of(j * tn, tn), tn)] = alpha * x_ref[...] + acc

    @pl.when(j == pl.num_programs(1) - 1)
    def _():
        out = _layer_norm_rows(o_ref[...], g_ref[...], b_ref[...])
        o_ref[...] = out
        op_ref[...] = _pack_bf16_pairs(out)


def _out_proj_ln(o_f, o_m, w, layer, x, g, b, *, alpha):
    n, d = x.shape
    fox_w, moba_w = o_f.shape[1], o_m.shape[1]
    tm = min(512, n)
    tn = min(512, d)
    kern = functools.partial(_out_proj_kernel, tn=tn, fox_w=fox_w, alpha=alpha)
    return pl.pallas_call(
        kern,
        grid=(n // tm, d // tn),
        in_specs=[
            pl.BlockSpec((tm, fox_w), lambda i, j: (i, 0)),
            pl.BlockSpec((tm, moba_w), lambda i, j: (i, 0)),
            pl.BlockSpec((1, fox_w + moba_w, tn), lambda i, j: (layer, 0, j)),
            pl.BlockSpec((tm, tn), lambda i, j: (i, j)),
            pl.BlockSpec((1, d), lambda i, j: (0, 0)),
            pl.BlockSpec((1, d), lambda i, j: (0, 0)),
        ],
        out_specs=[pl.BlockSpec((tm, d), lambda i, j: (i, 0)),
                   pl.BlockSpec((tm, d // 2), lambda i, j: (i, 0))],
        out_shape=[jax.ShapeDtypeStruct((n, d), F32), jax.ShapeDtypeStruct((n, d // 2), jnp.uint32)],
        compiler_params=_params("arbitrary", "arbitrary"),
        name="out_proj_ln",
    )(o_f, o_m, w, x, g, b)


def _router_kernel(x_ref, wt_ref, bias_ref, idx_ref, wts_ref, *, n_exp):
    x = x_ref[...]
    x_hi = x.astype(BF16)
    x_lo = (x - x_hi.astype(F32)).astype(BF16)
    w = wt_ref[...]
    w_hi = w.astype(BF16)
    w_lo = (w - w_hi.astype(F32)).astype(BF16)
    part = lax.dot_general(jnp.concatenate([w_hi, w_lo], axis=0), x_hi, NT_DIMS, preferred_element_type=F32)
    logits = (part[:n_exp] + part[n_exp:]
              + lax.dot_general(w_hi, x_lo, NT_DIMS, preferred_element_type=F32))
    ex = jnp.exp(logits - logits.max(axis=0, keepdims=True))
    probs = ex / ex.sum(axis=0, keepdims=True)
    sel = probs + bias_ref[...]
    per = n_exp // N_GROUPS
    gscore = []
    for gidx in range(N_GROUPS):
        r = [sel[gidx * per + a:gidx * per + a + 1, :] for a in range(per)]
        best = None
        for a in range(per):
            for b in range(a + 1, per):
                pair = r[a] + r[b]
                best = pair if best is None else jnp.maximum(best, pair)
        gscore.append(best)
    gmax = functools.reduce(jnp.maximum, gscore)
    g_idx = jnp.full_like(gmax, N_GROUPS - 1).astype(jnp.int32)
    for gidx in range(N_GROUPS - 2, -1, -1):
        g_idx = jnp.where(gscore[gidx] == gmax, gidx, g_idx)
    erow = lax.broadcasted_iota(jnp.int32, sel.shape, 0)
    masked = jnp.where(erow // per == g_idx, sel, NEG_INF)
    v1 = masked.max(axis=0, keepdims=True)
    i1 = jnp.min(jnp.where(masked == v1, erow, n_exp), axis=0, keepdims=True)
    masked2 = jnp.where(erow == i1, -jnp.inf, masked)
    v2 = masked2.max(axis=0, keepdims=True)
    i2 = jnp.min(jnp.where(masked2 == v2, erow, n_exp), axis=0, keepdims=True)
    w1 = jnp.sum(jnp.where(erow == i1, probs, 0.0), axis=0, keepdims=True)
    w2 = jnp.sum(jnp.where(erow == i2, probs, 0.0), axis=0, keepdims=True)
    tot = w1 + w2
    idx_ref[...] = jnp.concatenate([i1, i2], axis=0)
    wts_ref[...] = jnp.concatenate([w1 / tot, w2 / tot], axis=0)


def _router(x, w_router_t, bias_col):
    n, d = x.shape
    n_exp = w_router_t.shape[0]
    tm = min(512, n)
    return pl.pallas_call(
        functools.partial(_router_kernel, n_exp=n_exp),
        grid=(n // tm,),
        in_specs=[
            pl.BlockSpec((tm, d), lambda i: (i, 0)),
            pl.BlockSpec((n_exp, d), lambda i: (0, 0)),
            pl.BlockSpec((n_exp, 1), lambda i: (0, 0)),
        ],
        out_specs=[
            pl.BlockSpec((TOP_K, tm), lambda i: (0, i)),
            pl.BlockSpec((TOP_K, tm), lambda i: (0, i)),
        ],
        out_shape=[
            jax.ShapeDtypeStruct((TOP_K, n), jnp.int32),
            jax.ShapeDtypeStruct((TOP_K, n), F32),
        ],
        compiler_params=_params("arbitrary"),
        name="router",
    )(x, w_router_t, bias_col)


def _dispatch(idx, n_exp):
    k, n = idx.shape
    e = idx.reshape(-1)
    onehot = (e[:, None] == jnp.arange(n_exp, dtype=jnp.int32)[None, :]).astype(jnp.int32)
    cums = jnp.cumsum(onehot, axis=0)
    rank = jnp.take_along_axis(cums, e[:, None], axis=1)[:, 0] - 1
    counts = cums[-1]
    padded = (counts + MOE_ROWS - 1) // MOE_ROWS * MOE_ROWS
    pend = jnp.cumsum(padded)
    dest = (pend - padded)[e] + rank
    n_blocks = (k * n) // MOE_ROWS + n_exp
    tok = jnp.tile(jnp.arange(n, dtype=jnp.int32), k)
    row_tok = jnp.zeros((n_blocks * MOE_ROWS,), jnp.int32).at[dest].set(tok)
    blk_start = jnp.arange(n_blocks, dtype=jnp.int32) * MOE_ROWS
    blk_expert = jnp.minimum(
        jnp.sum((pend[None, :] <= blk_start[:, None]).astype(jnp.int32), axis=1), n_exp - 1)
    n_used = (pend[-1:] // MOE_ROWS).astype(jnp.int32)
    used = jnp.arange(n_blocks, dtype=jnp.int32) < n_used[0]
    later = used[None, :] & (blk_expert[None, :] > blk_expert[:, None])
    next_expert = jnp.min(jnp.where(later, blk_expert[None, :], n_exp), axis=1)
    next_expert = jnp.where(next_expert >= n_exp, -1, next_expert).astype(jnp.int32)
    return dest.astype(jnp.int32), row_tok, blk_expert, next_expert, n_used


def _start_row_gather(idx_ref, base, count, src_hbm, dst_ref, sem):
    for i in range(count):
        r = idx_ref[base + i]
        pltpu.make_async_copy(src_hbm.at[pl.ds(r, 1), :], dst_ref.at[pl.ds(i, 1), :], sem).start()


def _wait_row_gather(count, src_hbm, dst_ref, sem):
    pltpu.make_async_copy(src_hbm.at[pl.ds(0, count), :], dst_ref, sem).wait()


def _start_row_gather_loop(idx_ref, base, count, src_hbm, dst_ref, sem):
    def issue(i, c):
        r = idx_ref[base + i]
        pltpu.make_async_copy(src_hbm.at[pl.ds(r, 1), :], dst_ref.at[pl.ds(i, 1), :], sem).start()
        return c
    lax.fori_loop(0, count, issue, 0, unroll=8)


def _moe_up_kernel(blk_ref, nxt_exp_ref, tok_ref, nused_ref, xp_hbm, wg_hbm, wu_hbm, h_ref,
                   xg_ref, wf_ref, wgb_ref, wub_ref, sems, wsems, *, layer):
    b = pl.program_id(0)
    n_used = nused_ref[0]
    last = n_used - 1

    def weight_copies(e):
        return (pltpu.make_async_copy(wg_hbm.at[layer, e], wf_ref.at[0], wsems.at[0]),
                pltpu.make_async_copy(wu_hbm.at[layer, e], wf_ref.at[1], wsems.at[1]))

    @pl.when((b == 0) & (n_used > 0))
    def _():
        for c in weight_copies(blk_ref[0]):
            c.start()
        _start_row_gather_loop(tok_ref, 0, MOE_ROWS, xp_hbm, xg_ref.at[0], sems.at[0])
        _start_row_gather_loop(tok_ref, jnp.minimum(1, last) * MOE_ROWS, MOE_ROWS, xp_hbm,
                               xg_ref.at[1], sems.at[1])

    @pl.when((b < n_used) & ((b == 0) | (blk_ref[b] != blk_ref[jnp.maximum(b - 1, 0)])))
    def _():
        for c in weight_copies(blk_ref[b]):
            c.wait()
        def cast_rows(c, carry):
            sl = pl.ds(pl.multiple_of(c * CAST_ROWS, CAST_ROWS), CAST_ROWS)
            wgb_ref[sl, :] = wf_ref[0, sl, :].astype(BF16)
            wub_ref[sl, :] = wf_ref[1, sl, :].astype(BF16)
            return carry
        lax.fori_loop(0, wgb_ref.shape[0] // CAST_ROWS, cast_rows, 0)

        @pl.when(nxt_exp_ref[b] >= 0)
        def _():
            for c in weight_copies(nxt_exp_ref[b]):
                c.start()

    @pl.when(b < n_used)
    def _():
        slot = b % 3
        ahead = (b + 2) % 3
        _wait_row_gather(MOE_ROWS, xp_hbm, xg_ref.at[slot], sems.at[slot])

        ahead_base = jnp.minimum(b + 2, last) * MOE_ROWS
        xb = jnp.concatenate(_unpack_bf16_pairs(xg_ref[slot]), axis=1)
        chunk = min(FF_CHUNK, h_ref.shape[1])
        n_chunks = h_ref.shape[1] // chunk
        per = -(-MOE_ROWS // n_chunks)
        for c in range(n_chunks):
            for i in range(c * per, min((c + 1) * per, MOE_ROWS)):
                r = tok_ref[ahead_base + i]
                pltpu.make_async_copy(xp_hbm.at[pl.ds(r, 1), :], xg_ref.at[ahead, pl.ds(i, 1), :],
                                      sems.at[ahead]).start()
            sl = slice(c * chunk, (c + 1) * chunk)
            g = jnp.dot(xb, wgb_ref[:, sl], preferred_element_type=F32)
            u = jnp.dot(xb, wub_ref[:, sl], preferred_element_type=F32)
            h_ref[:, sl] = (g * jax.nn.sigmoid(g) * u).astype(BF16)

        @pl.when(b == last)
        def _():
            other = (b + 1) % 3
            _wait_row_gather(MOE_ROWS, xp_hbm, xg_ref.at[other], sems.at[other])
            _wait_row_gather(MOE_ROWS, xp_hbm, xg_ref.at[ahead], sems.at[ahead])

    @pl.when(b >= n_used)
    def _():
        h_ref[...] = jnp.zeros_like(h_ref)


def _moe_up(xp, w_gate, w_up, layer, blk_expert, next_expert, row_tok, n_used):
    n, dp = xp.shape
    n_blocks = blk_expert.shape[0]
    d, d_ff = w_gate.shape[-2:]
    return pl.pallas_call(
        functools.partial(_moe_up_kernel, layer=layer),
        grid_spec=pltpu.PrefetchScalarGridSpec(
            num_scalar_prefetch=4,
            grid=(n_blocks,),
            in_specs=[
                pl.BlockSpec(memory_space=pl.ANY),
                pl.BlockSpec(memory_space=pl.ANY),
                pl.BlockSpec(memory_space=pl.ANY),
            ],
            out_specs=pl.BlockSpec((MOE_ROWS, d_ff), lambda b, blk, nxt, tok, nu: (b, 0)),
            scratch_shapes=[pltpu.VMEM((3, MOE_ROWS, dp), jnp.uint32),
                            pltpu.VMEM((2, d, d_ff), F32), pltpu.VMEM((d, d_ff), BF16), pltpu.VMEM((d, d_ff), BF16),
                            pltpu.SemaphoreType.DMA((3,)), pltpu.SemaphoreType.DMA((2,))],
        ),
        out_shape=jax.ShapeDtypeStruct((n_blocks * MOE_ROWS, d_ff), BF16),
        compiler_params=_params("arbitrary"),
        name="moe_up",
    )(blk_expert, next_expert, row_tok, n_used, xp, w_gate, w_up)


def _moe_down_kernel(blk_ref, nxt_exp_ref, nused_ref, h_ref, wd_hbm, y_ref, wf_ref, wb_ref, wsem, *, layer):
    b = pl.program_id(0)
    new_expert = (b == 0) | (blk_ref[b] != blk_ref[jnp.maximum(b - 1, 0)])

    def weight_copy(e):
        return pltpu.make_async_copy(wd_hbm.at[layer, e], wf_ref, wsem)

    @pl.when((b == 0) & (nused_ref[0] > 0))
    def _():
        weight_copy(blk_ref[0]).start()

    @pl.when((b < nused_ref[0]) & new_expert)
    def _():
        weight_copy(blk_ref[b]).wait()

        rows = min(CAST_ROWS, wb_ref.shape[0])

        def cast_rows(c, carry):
            sl = pl.ds(pl.multiple_of(c * rows, rows), rows)
            wb_ref[sl, :] = wf_ref[sl, :].astype(BF16)
            return carry
        lax.fori_loop(0, wb_ref.shape[0] // rows, cast_rows, 0)

        @pl.when(nxt_exp_ref[b] >= 0)
        def _():
            weight_copy(nxt_exp_ref[b]).start()

    @pl.when(b < nused_ref[0])
    def _():
        y_ref[...] = _pack_bf16_pairs(jnp.dot(h_ref[...], wb_ref[...], preferred_element_type=F32))

    @pl.when(b >= nused_ref[0])
    def _():
        y_ref[...] = jnp.zeros_like(y_ref)


def _moe_down(h, w_down, layer, blk_expert, next_expert, n_used):
    n_blocks = blk_expert.shape[0]
    d_ff, d = w_down.shape[2:]
    return pl.pallas_call(
        functools.partial(_moe_down_kernel, layer=layer),
        grid_spec=pltpu.PrefetchScalarGridSpec(
            num_scalar_prefetch=3,
            grid=(n_blocks,),
            in_specs=[
                pl.BlockSpec((MOE_ROWS, d_ff), lambda b, blk, nxt, nu: (b, 0)),
                pl.BlockSpec(memory_space=pl.ANY),
            ],
            out_specs=pl.BlockSpec((MOE_ROWS, d // 2), lambda b, blk, nxt, nu: (b, 0)),
            scratch_shapes=[pltpu.VMEM((d_ff, d), F32), pltpu.VMEM((d_ff, d), BF16), pltpu.SemaphoreType.DMA(())],
        ),
        out_shape=jax.ShapeDtypeStruct((n_blocks * MOE_ROWS, d // 2), jnp.uint32),
        compiler_params=_params("arbitrary"),
        name="moe_down",
    )(blk_expert, next_expert, n_used, h, w_down)


def _combine_kernel(dest_ref, y_hbm, x_ref, wt_ref, g_ref, b_ref, o_ref, ob_ref,
                    ya0_ref, yb0_ref, ya1_ref, yb1_ref, sems, *, tc, n_tok, alpha):
    i = pl.program_id(0)
    last = pl.num_programs(0) - 1
    nxt_base = jnp.minimum(i + 1, last) * tc

    def start(base, ya_ref, yb_ref, slot):
        _start_row_gather(dest_ref, base, tc, y_hbm, ya_ref, sems.at[slot, 0])
        _start_row_gather(dest_ref, n_tok + base, tc, y_hbm, yb_ref, sems.at[slot, 1])

    def wait(ya_ref, yb_ref, slot):
        _wait_row_gather(tc, y_hbm, ya_ref, sems.at[slot, 0])
        _wait_row_gather(tc, y_hbm, yb_ref, sems.at[slot, 1])

    @pl.when(i == 0)
    def _():
        start(0, ya0_ref, yb0_ref, 0)

    def step(cur, nxt):
        wait(*cur)
        start(nxt_base, *nxt)
        ya = jnp.concatenate(_unpack_pairs_f32(cur[0][...]), axis=1)
        yb = jnp.concatenate(_unpack_pairs_f32(cur[1][...]), axis=1)
        moe = wt_ref[:, 0:1] * ya + wt_ref[:, 1:2] * yb
        out = _layer_norm_rows(alpha * x_ref[...] + moe, g_ref[...], b_ref[...])
        o_ref[...] = out
        ob_ref[...] = out.astype(BF16)

    slot0 = (ya0_ref, yb0_ref, 0)
    slot1 = (ya1_ref, yb1_ref, 1)

    @pl.when(i % 2 == 0)
    def _():
        step(slot0, slot1)

    @pl.when(i % 2 == 1)
    def _():
        step(slot1, slot0)

    @pl.when(i == last)
    def _():
        @pl.when(i % 2 == 0)
        def _():
            wait(*slot1)

        @pl.when(i % 2 == 1)
        def _():
            wait(*slot0)


def _combine_ln(y, x, wts_tok, dest, g, b, *, alpha):
    n, d = x.shape
    tc = min(256, n)
    kern = functools.partial(_combine_kernel, tc=tc, n_tok=n, alpha=alpha)
    return pl.pallas_call(
        kern,
        grid_spec=pltpu.PrefetchScalarGridSpec(
            num_scalar_prefetch=1,
            grid=(n // tc,),
            in_specs=[
                pl.BlockSpec(memory_space=pl.ANY),
                pl.BlockSpec((tc, d), lambda i, dest: (i, 0)),
                pl.BlockSpec((tc, TOP_K), lambda i, dest: (i, 0)),
                pl.BlockSpec((1, d), lambda i, dest: (0, 0)),
                pl.BlockSpec((1, d), lambda i, dest: (0, 0)),
            ],
            out_specs=[
                pl.BlockSpec((tc, d), lambda i, dest: (i, 0)),
                pl.BlockSpec((tc, d), lambda i, dest: (i, 0)),
            ],
            scratch_shapes=[pltpu.VMEM((tc, d // 2), jnp.uint32)] * 4 + [pltpu.SemaphoreType.DMA((2, 2))],
        ),
        out_shape=[jax.ShapeDtypeStruct((n, d), F32), jax.ShapeDtypeStruct((n, d), BF16)],
        compiler_params=_params("arbitrary"),
        name="combine_ln",
    )(dest, y, x, wts_tok, g, b)


def _rope_tables(seq):
    half = HEAD_DIM // 2
    inv = ROPE_THETA ** (-jnp.arange(half, dtype=F32) / half)
    ang = jnp.arange(seq, dtype=jnp.int32).astype(F32)[:, None] * inv[None, :]
    cos, sin = jnp.cos(ang), jnp.sin(ang)
    return jnp.concatenate([cos, cos], axis=-1), jnp.concatenate([-sin, sin], axis=-1)


def kernel(x, w_in, b_forget, w_out, ln1_g, ln1_b, w_router, router_bias, w_gate, w_up, w_down, ln2_g, ln2_b):
    batch, seq, d = x.shape
    depth = w_in.shape[0]
    fox_heads = b_forget.shape[-1]
    fox_w = fox_heads * HEAD_DIM
    moba_w = w_out.shape[1] - fox_w
    moba_heads = moba_w // HEAD_DIM
    n_exp = w_router.shape[1]
    n = batch * seq
    alpha = (2.0 * depth) ** 0.25
    qkv_w = 3 * (fox_w + moba_w)
    hb = HEAD_DIM

    cos, sin = _rope_tables(seq)
    w_router_t = w_router.T
    bias_col = router_bias.reshape(n_exp, 1)
    xf = x.reshape(n, d)
    xb = xf.astype(BF16)
    w_in_t = jnp.swapaxes(w_in, 1, 2)
    w_out_b = w_out.astype(BF16)

    for l in range(depth):
        w_f = jnp.pad(w_in_t[l, qkv_w:, :], ((0, LANES - fox_heads), (0, 0)))
        b_f = jnp.pad(b_forget[l], (0, LANES - fox_heads)).reshape(1, LANES)

        proj, kbar = _in_proj(xb, w_in_t, l, cos, sin, fox_w=fox_w, moba_w=moba_w, seq=seq)
        cp = _forget_cumsum(xb, w_f, b_f, batch=batch, seq=seq, heads=fox_heads)
        o_f = _fox_attention(proj, cp, batch=batch, seq=seq, heads=fox_heads,
                             k_col=fox_w // hb, v_col=2 * fox_w // hb)
        o_m = _moba_attention(proj, kbar.reshape(batch, seq // MOBA_BLOCK, moba_w),
                              batch=batch, seq=seq, heads=moba_heads,
                              q_col=3 * fox_w // hb, k_col=(3 * fox_w + moba_w) // hb,
                              v_col=(3 * fox_w + 2 * moba_w) // hb)
        x1, x1p = _out_proj_ln(o_f, o_m, w_out_b, l, xf,
                               ln1_g[l].reshape(1, d), ln1_b[l].reshape(1, d), alpha=alpha)

        idx, wts = _router(x1, w_router_t, bias_col)
        dest, row_tok, blk_expert, next_expert, n_used = _dispatch(idx, n_exp)
        h = _moe_up(x1p, w_gate, w_up, l, blk_expert, next_expert, row_tok, n_used)
        y = _moe_down(h, w_down, l, blk_expert, next_expert, n_used)
        xf, xb = _combine_ln(y, x1, wts.T, dest, ln2_g[l].reshape(1, d), ln2_b[l].reshape(1, d),
                             alpha=alpha)

    return xf.reshape(batch, seq, d)
```

```python
import functools
import math

import jax
import jax.numpy as jnp
from jax import lax
from jax.experimental import pallas as pl
from jax.experimental.pallas import tpu as pltpu

HEAD_DIM = 128
MOBA_BLOCK = 256
MOBA_TOPK = 3
ROPE_THETA = 10000.0
N_GROUPS = 4
TOP_K = 2
LN_EPS = 1e-5
NEG_INF = -1e30
LOG2E = math.log2(math.e)
Q_SCALE = HEAD_DIM ** -0.5 * LOG2E
MOE_ROWS = 256
CAST_ROWS = 256
FF_CHUNK = 256
LANES = 128
VMEM_LIMIT = 56 * 1024 * 1024
ATTN_TILE = 2048
ATTN_KEY_TILE = 1024
ATTN_SUB = 256

F32 = jnp.float32
BF16 = jnp.bfloat16
NT_DIMS = (((1,), (1,)), ((), ()))


def _params(*sem):
    return pltpu.CompilerParams(dimension_semantics=sem, vmem_limit_bytes=VMEM_LIMIT)


def _split3(a):
    hi = a.astype(BF16)
    r1 = a - hi.astype(F32)
    mid = r1.astype(BF16)
    lo = (r1 - mid.astype(F32)).astype(BF16)
    return hi, mid, lo


def _in_proj_kernel(x_ref, w_ref, cos_ref, sin_ref, o_ref, kbar_ref, *, tm, tn, j0, j1, j2, j3):
    acc = lax.dot_general(x_ref[...], w_ref[0].astype(BF16), NT_DIMS, preferred_element_type=F32)
    j = pl.program_id(1)

    @pl.when(j < j0)
    def _():
        o_ref[...] = (acc * Q_SCALE).astype(BF16)

    @pl.when(((j >= j0) & (j < j1)) | (j >= j3))
    def _():
        o_ref[...] = acc.astype(BF16)

    def roped(c):
        ch = acc[:, c * HEAD_DIM:(c + 1) * HEAD_DIM]
        return ch * cos_ref[...] + pltpu.roll(ch, HEAD_DIM // 2, 1) * sin_ref[...]

    @pl.when((j >= j1) & (j < j2))
    def _():
        for c in range(tn // HEAD_DIM):
            o_ref[:, c * HEAD_DIM:(c + 1) * HEAD_DIM] = (roped(c) * Q_SCALE).astype(BF16)

    @pl.when((j >= j2) & (j < j3))
    def _():
        for c in range(tn // HEAD_DIM):
            r = roped(c)
            o_ref[:, c * HEAD_DIM:(c + 1) * HEAD_DIM] = r.astype(BF16)
            kbar_ref[0, :, c * HEAD_DIM:(c + 1) * HEAD_DIM] = (
                r.reshape(tm // MOBA_BLOCK, MOBA_BLOCK, HEAD_DIM).sum(axis=1) * (1.0 / MOBA_BLOCK))


def _in_proj(xb, w_in_t, layer, cos, sin, *, fox_w, moba_w, seq):
    n, d = xb.shape
    tm = min(1024, seq)
    tn = min(512, fox_w, moba_w)
    fq, mq = fox_w // tn, moba_w // tn
    j0, j1 = fq, 3 * fq
    j2, j3 = j1 + mq, j1 + 2 * mq
    nj = j1 + 3 * mq
    spb = seq // tm
    kern = functools.partial(_in_proj_kernel, tm=tm, tn=tn, j0=j0, j1=j1, j2=j2, j3=j3)
    return pl.pallas_call(
        kern,
        grid=(n // tm, nj),
        in_specs=[
            pl.BlockSpec((tm, d), lambda i, j: (i, 0)),
            pl.BlockSpec((1, tn, d), lambda i, j: (layer, j, 0)),
            pl.BlockSpec((tm, HEAD_DIM), lambda i, j: (i % spb, 0)),
            pl.BlockSpec((tm, HEAD_DIM), lambda i, j: (i % spb, 0)),
        ],
        out_specs=[
            pl.BlockSpec((tm, tn), lambda i, j: (i, j)),
            pl.BlockSpec((1, tm // MOBA_BLOCK, tn),
                         lambda i, j: (i, 0, jnp.clip(j - j2, 0, mq - 1))),
        ],
        out_shape=[
            jax.ShapeDtypeStruct((n, 3 * (fox_w + moba_w)), BF16),
            jax.ShapeDtypeStruct((n // tm, tm // MOBA_BLOCK, moba_w), F32),
        ],
        compiler_params=_params("arbitrary", "arbitrary"),
        name="in_proj",
    )(xb, w_in_t, cos, sin)


def _forget_kernel(x_ref, w_ref, b_ref, cp_ref, carry_ref, *, ts, heads):
    @pl.when(pl.program_id(1) == 0)
    def _():
        carry_ref[...] = jnp.zeros_like(carry_ref)

    f = lax.dot_general(x_ref[...], w_ref[...].astype(BF16), NT_DIMS, preferred_element_type=F32) + b_ref[...]
    lf = jnp.minimum(f, 0.0) - jnp.log1p(jnp.exp(-jnp.abs(f)))
    row = lax.broadcasted_iota(jnp.int32, (ts, ts), 0)
    col = lax.broadcasted_iota(jnp.int32, (ts, ts), 1)
    tri = jnp.where(row >= col, 1.0, 0.0).astype(BF16)
    cs = carry_ref[...]
    for part in _split3(lf):
        cs = cs + jnp.dot(tri, part, preferred_element_type=F32)
    carry_ref[...] = cs[ts - 1:ts, :]

    lane = lax.broadcasted_iota(jnp.int32, (ts, LANES), 1)
    neg = jnp.where(lane < heads, -LOG2E * cs, 0.0)
    erow = lax.broadcasted_iota(jnp.int32, (LANES, LANES), 0)
    ecol = lax.broadcasted_iota(jnp.int32, (LANES, LANES), 1)
    out = jnp.zeros((ts, LANES), F32)
    for j, part in enumerate(_split3(neg)):
        spread = jnp.where(ecol == 3 * erow + j, 1.0, 0.0).astype(BF16)
        out = out + jnp.dot(part, spread, preferred_element_type=F32)
    cp_ref[...] = out.astype(BF16)


def _forget_cumsum(xb, w_f, b_f, *, batch, seq, heads):
    n, d = xb.shape
    assert 3 * heads <= LANES
    ts = min(512, seq)
    spb = seq // ts
    return pl.pallas_call(
        functools.partial(_forget_kernel, ts=ts, heads=heads),
        grid=(batch, spb),
        in_specs=[
            pl.BlockSpec((ts, d), lambda b, s: (b * spb + s, 0)),
            pl.BlockSpec((LANES, d), lambda b, s: (0, 0)),
            pl.BlockSpec((1, LANES), lambda b, s: (0, 0)),
        ],
        out_specs=pl.BlockSpec((ts, LANES), lambda b, s: (b * spb + s, 0)),
        out_shape=jax.ShapeDtypeStruct((n, LANES), BF16),
        scratch_shapes=[pltpu.VMEM((1, LANES), F32)],
        compiler_params=_params("arbitrary", "arbitrary"),
        name="forget_cumsum",
    )(xb, w_f, b_f)


def _softmax_update(m, l, s_t):
    m_new = jnp.maximum(m, s_t.max(axis=0, keepdims=True))
    alpha = jnp.exp2(m - m_new)
    p_t = jnp.exp2(s_t - m_new)
    return m_new, alpha * l + p_t.sum(axis=0, keepdims=True), alpha, p_t.astype(BF16)


LEAD = 4


def _attend_tile(q_subs, keys, values_t, masks, carry):
    n_sub = len(q_subs)

    def scores(r):
        s_t = jnp.dot(keys[r], q_subs[r], preferred_element_type=F32)
        if masks[r] is None:
            return s_t
        rows = masks[r].shape[0]
        tail = jnp.where(masks[r], s_t[s_t.shape[0] - rows:], NEG_INF)
        return tail if rows == s_t.shape[0] else jnp.concatenate([s_t[:s_t.shape[0] - rows], tail], axis=0)

    s = {r: scores(r) for r in range(min(LEAD, n_sub))}
    out = []
    for r in range(n_sub):
        m, l, acc_t = carry[r]
        m, l, alpha, p_t = _softmax_update(m, l, s.pop(r))
        if r + LEAD < n_sub:
            s[r + LEAD] = scores(r + LEAD)
        acc_t = alpha * acc_t + jnp.dot(values_t[r], p_t, preferred_element_type=F32)
        out.append((m, l, acc_t))
    return tuple(out)


def _transpose_bf16(a):
    return a.astype(F32).T.astype(BF16)


def _store_keys_values(k_ref, v_ref, kaug_ref, vt_ref, aug, *, chunk):
    seq = k_ref.shape[0]
    kaug_ref[:, :HEAD_DIM] = k_ref[...]
    kaug_ref[:, HEAD_DIM:] = aug
    for c in range(seq // chunk):
        vt_ref[:, c * chunk:(c + 1) * chunk] = _transpose_bf16(v_ref[c * chunk:(c + 1) * chunk, :])


def _flash_attention(qt_aug, kaug_ref, vt_ref, o_ref, qi, *, t, sub):
    n_sub = t // sub
    q_subs = [qt_aug[:, r * sub:(r + 1) * sub] for r in range(n_sub)]

    tk = min(ATTN_KEY_TILE, t)

    def past(ki, carry):
        start = pl.multiple_of(ki * tk, tk)
        k = kaug_ref[pl.ds(start, tk), :]
        v_t = vt_ref[:, pl.ds(start, tk)]
        return _attend_tile(q_subs, [k] * n_sub, [v_t] * n_sub, [None] * n_sub, carry)

    init = tuple((jnp.full((1, sub), -jnp.inf, F32), jnp.zeros((1, sub), F32), jnp.zeros((HEAD_DIM, sub), F32))
                 for _ in range(n_sub))
    carry = lax.fori_loop(0, qi * (t // tk), past, init)

    start = pl.multiple_of(qi * t, t)
    keys, values_t = [], []
    for r in range(n_sub):
        width = (r + 1) * sub
        keys.append(kaug_ref[pl.ds(start, width), :])
        values_t.append(vt_ref[:, pl.ds(start, width)])
    causal = (lax.broadcasted_iota(jnp.int32, (sub, sub), 0) <= lax.broadcasted_iota(jnp.int32, (sub, sub), 1))
    masks = [causal] * n_sub
    carry = _attend_tile(q_subs, keys, values_t, masks, carry)
    for r in range(n_sub):
        _, l, acc_t = carry[r]
        o_ref[r * sub:(r + 1) * sub, :] = (acc_t / l).T.astype(BF16)


def _fox_kernel(q_ref, k_ref, v_ref, cp_ref, o_ref, kaug_ref, vt_ref, *, t, sub):
    h = pl.program_id(1)
    qi = pl.program_id(2)

    @pl.when(qi == 0)
    def _():
        lane = lax.broadcasted_iota(jnp.int32, cp_ref.shape, 1)
        mine = (lane >= 3 * h) & (lane < 3 * h + 3)
        _store_keys_values(k_ref, v_ref, kaug_ref, vt_ref,
                           jnp.where(mine, cp_ref[...], jnp.zeros_like(cp_ref)), chunk=t)

    row = lax.broadcasted_iota(jnp.int32, (LANES, t), 0)
    ones_t = jnp.where((row >= 3 * h) & (row < 3 * h + 3), 1.0, 0.0).astype(BF16)
    qt_aug = jnp.concatenate([_transpose_bf16(q_ref[...]), ones_t], axis=0)
    _flash_attention(qt_aug, kaug_ref, vt_ref, o_ref, qi, t=t, sub=sub)


def _fox_attention(proj, cp, *, batch, seq, heads, k_col, v_col):
    n = proj.shape[0]
    t = min(ATTN_TILE, seq)
    sub = min(ATTN_SUB, t)
    nq = seq // t
    return pl.pallas_call(
        functools.partial(_fox_kernel, t=t, sub=sub),
        grid=(batch, heads, nq),
        in_specs=[
            pl.BlockSpec((t, HEAD_DIM), lambda b, h, qi: (b * nq + qi, h)),
            pl.BlockSpec((seq, HEAD_DIM), lambda b, h, qi: (b, k_col + h)),
            pl.BlockSpec((seq, HEAD_DIM), lambda b, h, qi: (b, v_col + h)),
            pl.BlockSpec((seq, LANES), lambda b, h, qi: (b, 0)),
        ],
        out_specs=pl.BlockSpec((t, HEAD_DIM), lambda b, h, qi: (b * nq + qi, h)),
        out_shape=jax.ShapeDtypeStruct((n, heads * HEAD_DIM), BF16),
        scratch_shapes=[pltpu.VMEM((seq, 2 * HEAD_DIM), BF16), pltpu.VMEM((HEAD_DIM, seq), BF16)],
        compiler_params=_params("arbitrary", "arbitrary", "arbitrary"),
        name="fox_attention",
    )(proj, proj, proj, cp)


def _moba_kernel(q_ref, k_ref, v_ref, kbar_ref, o_ref, kaug_ref, vt_ref, *, t, sub, nb):
    qi = pl.program_id(2)
    seq = k_ref.shape[0]

    @pl.when(qi == 0)
    def _():
        r = lax.broadcasted_iota(jnp.int32, (seq, LANES), 0)
        c = lax.broadcasted_iota(jnp.int32, (seq, LANES), 1)
        _store_keys_values(k_ref, v_ref, kaug_ref, vt_ref,
                           jnp.where(r // MOBA_BLOCK == c, 1.0, 0.0).astype(BF16), chunk=t)

    q = q_ref[...]
    gate = jnp.zeros((nb, t), F32)
    for part in _split3(kbar_ref[0]):
        gate = gate + lax.dot_general(part, q, NT_DIMS, preferred_element_type=F32)
    blk = lax.broadcasted_iota(jnp.int32, (nb, t), 0)
    own = qi * (t // MOBA_BLOCK) + lax.broadcasted_iota(jnp.int32, (nb, t), 1) // MOBA_BLOCK
    valid = blk < own
    g = jnp.where(valid, gate, NEG_INF)
    chosen = blk < 0
    for _ in range(MOBA_TOPK):
        best = jnp.max(g, axis=0, keepdims=True)
        pick = jnp.min(jnp.where(g == best, blk, nb), axis=0, keepdims=True)
        hit = blk == pick
        chosen = chosen | hit
        g = jnp.where(hit, -jnp.inf, g)
    allowed = (valid & chosen) | (blk == own)
    bias_t = jnp.where(allowed, 0.0, NEG_INF).astype(BF16)
    qt_aug = jnp.concatenate([_transpose_bf16(q), bias_t, jnp.zeros((LANES - nb, t), BF16)], axis=0)
    _flash_attention(qt_aug, kaug_ref, vt_ref, o_ref, qi, t=t, sub=sub)


def _moba_attention(proj, kbar, *, batch, seq, heads, q_col, k_col, v_col):
    n = proj.shape[0]
    nb = seq // MOBA_BLOCK
    assert nb <= LANES
    t = min(ATTN_TILE, seq)
    nq = seq // t
    return pl.pallas_call(
        functools.partial(_moba_kernel, t=t, sub=min(ATTN_SUB, t), nb=nb),
        grid=(batch, heads, nq),
        in_specs=[
            pl.BlockSpec((t, HEAD_DIM), lambda b, h, qi: (b * nq + qi, q_col + h)),
            pl.BlockSpec((seq, HEAD_DIM), lambda b, h, qi: (b, k_col + h)),
            pl.BlockSpec((seq, HEAD_DIM), lambda b, h, qi: (b, v_col + h)),
            pl.BlockSpec((1, nb, HEAD_DIM), lambda b, h, qi: (b, 0, h)),
        ],
        out_specs=pl.BlockSpec((t, HEAD_DIM), lambda b, h, qi: (b * nq + qi, h)),
        out_shape=jax.ShapeDtypeStruct((n, heads * HEAD_DIM), BF16),
        scratch_shapes=[pltpu.VMEM((seq, 2 * HEAD_DIM), BF16), pltpu.VMEM((HEAD_DIM, seq), BF16)],
        compiler_params=_params("arbitrary", "arbitrary", "arbitrary"),
        name="moba_attention",
    )(proj, proj, proj, kbar)


def _layer_norm_rows(z, g, b):
    mu = jnp.mean(z, axis=-1, keepdims=True)
    zc = z - mu
    var = jnp.mean(zc * zc, axis=-1, keepdims=True)
    return zc * lax.rsqrt(var + LN_EPS) * g + b


def _pack_bf16_pairs(x):
    half = x.shape[1] // 2
    lo = pltpu.bitcast(x[:, :half].astype(BF16).astype(F32), jnp.uint32)
    hi = pltpu.bitcast(x[:, half:].astype(BF16).astype(F32), jnp.uint32)
    return (lo >> 16) | (hi & jnp.uint32(0xFFFF0000))


def _unpack_pairs_f32(p):
    return pltpu.bitcast(p << 16, F32), pltpu.bitcast(p & jnp.uint32(0xFFFF0000), F32)


def _unpack_bf16_pairs(p):
    lo, hi = _unpack_pairs_f32(p)
    return lo.astype(BF16), hi.astype(BF16)


def _out_proj_kernel(of_ref, om_ref, w_ref, x_ref, g_ref, b_ref, o_ref, op_ref, *, tn, fox_w, alpha):
    j = pl.program_id(1)
    acc = jnp.dot(of_ref[...], w_ref[0, :fox_w, :], preferred_element_type=F32)
    acc = acc + jnp.dot(om_ref[...], w_ref[0, fox_w:, :], preferred_element_type=F32)
    o_ref[:, pl.ds(pl.multiple_of(j * tn, tn), tn)] = alpha * x_ref[...] + acc

    @pl.when(j == pl.num_programs(1) - 1)
    def _():
        out = _layer_norm_rows(o_ref[...], g_ref[...], b_ref[...])
        o_ref[...] = out
        op_ref[...] = _pack_bf16_pairs(out)


def _out_proj_ln(o_f, o_m, w, layer, x, g, b, *, alpha):
    n, d = x.shape
    fox_w, moba_w = o_f.shape[1], o_m.shape[1]
    tm = min(512, n)
    tn = min(512, d)
    kern = functools.partial(_out_proj_kernel, tn=tn, fox_w=fox_w, alpha=alpha)
    return pl.pallas_call(
        kern,
        grid=(n // tm, d // tn),
        in_specs=[
            pl.BlockSpec((tm, fox_w), lambda i, j: (i, 0)),
            pl.BlockSpec((tm, moba_w), lambda i, j: (i, 0)),
            pl.BlockSpec((1, fox_w + moba_w, tn), lambda i, j: (layer, 0, j)),
            pl.BlockSpec((tm, tn), lambda i, j: (i, j)),
            pl.BlockSpec((1, d), lambda i, j: (0, 0)),
            pl.BlockSpec((1, d), lambda i, j: (0, 0)),
        ],
        out_specs=[pl.BlockSpec((tm, d), lambda i, j: (i, 0)),
                   pl.BlockSpec((tm, d // 2), lambda i, j: (i, 0))],
        out_shape=[jax.ShapeDtypeStruct((n, d), F32), jax.ShapeDtypeStruct((n, d // 2), jnp.uint32)],
        compiler_params=_params("arbitrary", "arbitrary"),
        name="out_proj_ln",
    )(o_f, o_m, w, x, g, b)


def _router_kernel(x_ref, wt_ref, bias_ref, idx_ref, wts_ref, *, n_exp):
    x = x_ref[...]
    x_hi = x.astype(BF16)
    x_lo = (x - x_hi.astype(F32)).astype(BF16)
    w = wt_ref[...]
    w_hi = w.astype(BF16)
    w_lo = (w - w_hi.astype(F32)).astype(BF16)
    part = lax.dot_general(jnp.concatenate([w_hi, w_lo], axis=0), x_hi, NT_DIMS, preferred_element_type=F32)
    logits = (part[:n_exp] + part[n_exp:]
              + lax.dot_general(w_hi, x_lo, NT_DIMS, preferred_element_type=F32))
    ex = jnp.exp(logits - logits.max(axis=0, keepdims=True))
    probs = ex / ex.sum(axis=0, keepdims=True)
    sel = probs + bias_ref[...]
    per = n_exp // N_GROUPS
    gscore = []
    for gidx in range(N_GROUPS):
        r = [sel[gidx * per + a:gidx * per + a + 1, :] for a in range(per)]
        best = None
        for a in range(per):
            for b in range(a + 1, per):
                pair = r[a] + r[b]
                best = pair if best is None else jnp.maximum(best, pair)
        gscore.append(best)
    gmax = functools.reduce(jnp.maximum, gscore)
    g_idx = jnp.full_like(gmax, N_GROUPS - 1).astype(jnp.int32)
    for gidx in range(N_GROUPS - 2, -1, -1):
        g_idx = jnp.where(gscore[gidx] == gmax, gidx, g_idx)
    erow = lax.broadcasted_iota(jnp.int32, sel.shape, 0)
    masked = jnp.where(erow // per == g_idx, sel, NEG_INF)
    v1 = masked.max(axis=0, keepdims=True)
    i1 = jnp.min(jnp.where(masked == v1, erow, n_exp), axis=0, keepdims=True)
    masked2 = jnp.where(erow == i1, -jnp.inf, masked)
    v2 = masked2.max(axis=0, keepdims=True)
    i2 = jnp.min(jnp.where(masked2 == v2, erow, n_exp), axis=0, keepdims=True)
    w1 = jnp.sum(jnp.where(erow == i1, probs, 0.0), axis=0, keepdims=True)
    w2 = jnp.sum(jnp.where(erow == i2, probs, 0.0), axis=0, keepdims=True)
    tot = w1 + w2
    idx_ref[...] = jnp.concatenate([i1, i2], axis=0)
    wts_ref[...] = jnp.concatenate([w1 / tot, w2 / tot], axis=0)


def _router(x, w_router_t, bias_col):
    n, d = x.shape
    n_exp = w_router_t.shape[0]
    tm = min(512, n)
    return pl.pallas_call(
        functools.partial(_router_kernel, n_exp=n_exp),
        grid=(n // tm,),
        in_specs=[
            pl.BlockSpec((tm, d), lambda i: (i, 0)),
            pl.BlockSpec((n_exp, d), lambda i: (0, 0)),
            pl.BlockSpec((n_exp, 1), lambda i: (0, 0)),
        ],
        out_specs=[
            pl.BlockSpec((TOP_K, tm), lambda i: (0, i)),
            pl.BlockSpec((TOP_K, tm), lambda i: (0, i)),
        ],
        out_shape=[
            jax.ShapeDtypeStruct((TOP_K, n), jnp.int32),
            jax.ShapeDtypeStruct((TOP_K, n), F32),
        ],
        compiler_params=_params("arbitrary"),
        name="router",
    )(x, w_router_t, bias_col)


def _dispatch(idx, n_exp):
    k, n = idx.shape
    e = idx.reshape(-1)
    onehot = (e[:, None] == jnp.arange(n_exp, dtype=jnp.int32)[None, :]).astype(jnp.int32)
    cums = jnp.cumsum(onehot, axis=0)
    rank = jnp.take_along_axis(cums, e[:, None], axis=1)[:, 0] - 1
    counts = cums[-1]
    padded = (counts + MOE_ROWS - 1) // MOE_ROWS * MOE_ROWS
    pend = jnp.cumsum(padded)
    dest = (pend - padded)[e] + rank
    n_blocks = (k * n) // MOE_ROWS + n_exp
    tok = jnp.tile(jnp.arange(n, dtype=jnp.int32), k)
    row_tok = jnp.zeros((n_blocks * MOE_ROWS,), jnp.int32).at[dest].set(tok)
    blk_start = jnp.arange(n_blocks, dtype=jnp.int32) * MOE_ROWS
    blk_expert = jnp.minimum(
        jnp.sum((pend[None, :] <= blk_start[:, None]).astype(jnp.int32), axis=1), n_exp - 1)
    n_used = (pend[-1:] // MOE_ROWS).astype(jnp.int32)
    used = jnp.arange(n_blocks, dtype=jnp.int32) < n_used[0]
    later = used[None, :] & (blk_expert[None, :] > blk_expert[:, None])
    next_expert = jnp.min(jnp.where(later, blk_expert[None, :], n_exp), axis=1)
    next_expert = jnp.where(next_expert >= n_exp, -1, next_expert).astype(jnp.int32)
    return dest.astype(jnp.int32), row_tok, blk_expert, next_expert, n_used


def _start_row_gather(idx_ref, base, count, src_hbm, dst_ref, sem):
    for i in range(count):
        r = idx_ref[base + i]
        pltpu.make_async_copy(src_hbm.at[pl.ds(r, 1), :], dst_ref.at[pl.ds(i, 1), :], sem).start()


def _wait_row_gather(count, src_hbm, dst_ref, sem):
    pltpu.make_async_copy(src_hbm.at[pl.ds(0, count), :], dst_ref, sem).wait()


def _start_row_gather_loop(idx_ref, base, count, src_hbm, dst_ref, sem):
    def issue(i, c):
        r = idx_ref[base + i]
        pltpu.make_async_copy(src_hbm.at[pl.ds(r, 1), :], dst_ref.at[pl.ds(i, 1), :], sem).start()
        return c
    lax.fori_loop(0, count, issue, 0, unroll=8)


def _moe_up_kernel(blk_ref, nxt_exp_ref, tok_ref, nused_ref, xp_hbm, wg_hbm, wu_hbm, h_ref,
                   xg_ref, wf_ref, wgb_ref, wub_ref, sems, wsems, *, layer):
    b = pl.program_id(0)
    n_used = nused_ref[0]
    last = n_used - 1

    def weight_copies(e):
        return (pltpu.make_async_copy(wg_hbm.at[layer, e], wf_ref.at[0], wsems.at[0]),
                pltpu.make_async_copy(wu_hbm.at[layer, e], wf_ref.at[1], wsems.at[1]))

    @pl.when((b == 0) & (n_used > 0))
    def _():
        for c in weight_copies(blk_ref[0]):
            c.start()
        _start_row_gather_loop(tok_ref, 0, MOE_ROWS, xp_hbm, xg_ref.at[0], sems.at[0])
        _start_row_gather_loop(tok_ref, jnp.minimum(1, last) * MOE_ROWS, MOE_ROWS, xp_hbm,
                               xg_ref.at[1], sems.at[1])

    @pl.when((b < n_used) & ((b == 0) | (blk_ref[b] != blk_ref[jnp.maximum(b - 1, 0)])))
    def _():
        for c in weight_copies(blk_ref[b]):
            c.wait()
        def cast_rows(c, carry):
            sl = pl.ds(pl.multiple_of(c * CAST_ROWS, CAST_ROWS), CAST_ROWS)
            wgb_ref[sl, :] = wf_ref[0, sl, :].astype(BF16)
            wub_ref[sl, :] = wf_ref[1, sl, :].astype(BF16)
            return carry
        lax.fori_loop(0, wgb_ref.shape[0] // CAST_ROWS, cast_rows, 0)

        @pl.when(nxt_exp_ref[b] >= 0)
        def _():
            for c in weight_copies(nxt_exp_ref[b]):
                c.start()

    @pl.when(b < n_used)
    def _():
        slot = b % 3
        ahead = (b + 2) % 3
        _wait_row_gather(MOE_ROWS, xp_hbm, xg_ref.at[slot], sems.at[slot])

        ahead_base = jnp.minimum(b + 2, last) * MOE_ROWS
        xb = jnp.concatenate(_unpack_bf16_pairs(xg_ref[slot]), axis=1)
        chunk = min(FF_CHUNK, h_ref.shape[1])
        n_chunks = h_ref.shape[1] // chunk
        per = -(-MOE_ROWS // n_chunks)
        for c in range(n_chunks):
            for i in range(c * per, min((c + 1) * per, MOE_ROWS)):
                r = tok_ref[ahead_base + i]
                pltpu.make_async_copy(xp_hbm.at[pl.ds(r, 1), :], xg_ref.at[ahead, pl.ds(i, 1), :],
                                      sems.at[ahead]).start()
            sl = slice(c * chunk, (c + 1) * chunk)
            g = jnp.dot(xb, wgb_ref[:, sl], preferred_element_type=F32)
            u = jnp.dot(xb, wub_ref[:, sl], preferred_element_type=F32)
            h_ref[:, sl] = (g * jax.nn.sigmoid(g) * u).astype(BF16)

        @pl.when(b == last)
        def _():
            other = (b + 1) % 3
            _wait_row_gather(MOE_ROWS, xp_hbm, xg_ref.at[other], sems.at[other])
            _wait_row_gather(MOE_ROWS, xp_hbm, xg_ref.at[ahead], sems.at[ahead])

    @pl.when(b >= n_used)
    def _():
        h_ref[...] = jnp.zeros_like(h_ref)


def _moe_up(xp, w_gate, w_up, layer, blk_expert, next_expert, row_tok, n_used):
    n, dp = xp.shape
    n_blocks = blk_expert.shape[0]
    d, d_ff = w_gate.shape[-2:]
    return pl.pallas_call(
        functools.partial(_moe_up_kernel, layer=layer),
        grid_spec=pltpu.PrefetchScalarGridSpec(
            num_scalar_prefetch=4,
            grid=(n_blocks,),
            in_specs=[
                pl.BlockSpec(memory_space=pl.ANY),
                pl.BlockSpec(memory_space=pl.ANY),
                pl.BlockSpec(memory_space=pl.ANY),
            ],
            out_specs=pl.BlockSpec((MOE_ROWS, d_ff), lambda b, blk, nxt, tok, nu: (b, 0)),
            scratch_shapes=[pltpu.VMEM((3, MOE_ROWS, dp), jnp.uint32),
                            pltpu.VMEM((2, d, d_ff), F32), pltpu.VMEM((d, d_ff), BF16), pltpu.VMEM((d, d_ff), BF16),
                            pltpu.SemaphoreType.DMA((3,)), pltpu.SemaphoreType.DMA((2,))],
        ),
        out_shape=jax.ShapeDtypeStruct((n_blocks * MOE_ROWS, d_ff), BF16),
        compiler_params=_params("arbitrary"),
        name="moe_up",
    )(blk_expert, next_expert, row_tok, n_used, xp, w_gate, w_up)


def _moe_down_kernel(blk_ref, nxt_exp_ref, nused_ref, h_ref, wd_hbm, y_ref, wf_ref, wb_ref, wsem, *, layer):
    b = pl.program_id(0)
    new_expert = (b == 0) | (blk_ref[b] != blk_ref[jnp.maximum(b - 1, 0)])

    def weight_copy(e):
        return pltpu.make_async_copy(wd_hbm.at[layer, e], wf_ref, wsem)

    @pl.when((b == 0) & (nused_ref[0] > 0))
    def _():
        weight_copy(blk_ref[0]).start()

    @pl.when((b < nused_ref[0]) & new_expert)
    def _():
        weight_copy(blk_ref[b]).wait()

        rows = min(CAST_ROWS, wb_ref.shape[0])

        def cast_rows(c, carry):
            sl = pl.ds(pl.multiple_of(c * rows, rows), rows)
            wb_ref[sl, :] = wf_ref[sl, :].astype(BF16)
            return carry
        lax.fori_loop(0, wb_ref.shape[0] // rows, cast_rows, 0)

        @pl.when(nxt_exp_ref[b] >= 0)
        def _():
            weight_copy(nxt_exp_ref[b]).start()

    @pl.when(b < nused_ref[0])
    def _():
        y_ref[...] = _pack_bf16_pairs(jnp.dot(h_ref[...], wb_ref[...], preferred_element_type=F32))

    @pl.when(b >= nused_ref[0])
    def _():
        y_ref[...] = jnp.zeros_like(y_ref)


def _moe_down(h, w_down, layer, blk_expert, next_expert, n_used):
    n_blocks = blk_expert.shape[0]
    d_ff, d = w_down.shape[2:]
    return pl.pallas_call(
        functools.partial(_moe_down_kernel, layer=layer),
        grid_spec=pltpu.PrefetchScalarGridSpec(
            num_scalar_prefetch=3,
            grid=(n_blocks,),
            in_specs=[
                pl.BlockSpec((MOE_ROWS, d_ff), lambda b, blk, nxt, nu: (b, 0)),
                pl.BlockSpec(memory_space=pl.ANY),
            ],
            out_specs=pl.BlockSpec((MOE_ROWS, d // 2), lambda b, blk, nxt, nu: (b, 0)),
            scratch_shapes=[pltpu.VMEM((d_ff, d), F32), pltpu.VMEM((d_ff, d), BF16), pltpu.SemaphoreType.DMA(())],
        ),
        out_shape=jax.ShapeDtypeStruct((n_blocks * MOE_ROWS, d // 2), jnp.uint32),
        compiler_params=_params("arbitrary"),
        name="moe_down",
    )(blk_expert, next_expert, n_used, h, w_down)


def _combine_kernel(dest_ref, y_hbm, x_ref, wt_ref, g_ref, b_ref, o_ref, ob_ref,
                    ya0_ref, yb0_ref, ya1_ref, yb1_ref, sems, *, tc, n_tok, alpha):
    i = pl.program_id(0)
    last = pl.num_programs(0) - 1
    nxt_base = jnp.minimum(i + 1, last) * tc

    def start(base, ya_ref, yb_ref, slot):
        _start_row_gather(dest_ref, base, tc, y_hbm, ya_ref, sems.at[slot, 0])
        _start_row_gather(dest_ref, n_tok + base, tc, y_hbm, yb_ref, sems.at[slot, 1])

    def wait(ya_ref, yb_ref, slot):
        _wait_row_gather(tc, y_hbm, ya_ref, sems.at[slot, 0])
        _wait_row_gather(tc, y_hbm, yb_ref, sems.at[slot, 1])

    @pl.when(i == 0)
    def _():
        start(0, ya0_ref, yb0_ref, 0)

    def step(cur, nxt):
        wait(*cur)
        start(nxt_base, *nxt)
        ya = jnp.concatenate(_unpack_pairs_f32(cur[0][...]), axis=1)
        yb = jnp.concatenate(_unpack_pairs_f32(cur[1][...]), axis=1)
        moe = wt_ref[:, 0:1] * ya + wt_ref[:, 1:2] * yb
        out = _layer_norm_rows(alpha * x_ref[...] + moe, g_ref[...], b_ref[...])
        o_ref[...] = out
        ob_ref[...] = out.astype(BF16)

    slot0 = (ya0_ref, yb0_ref, 0)
    slot1 = (ya1_ref, yb1_ref, 1)

    @pl.when(i % 2 == 0)
    def _():
        step(slot0, slot1)

    @pl.when(i % 2 == 1)
    def _():
        step(slot1, slot0)

    @pl.when(i == last)
    def _():
        @pl.when(i % 2 == 0)
        def _():
            wait(*slot1)

        @pl.when(i % 2 == 1)
        def _():
            wait(*slot0)


def _combine_ln(y, x, wts_tok, dest, g, b, *, alpha):
    n, d = x.shape
    tc = min(256, n)
    kern = functools.partial(_combine_kernel, tc=tc, n_tok=n, alpha=alpha)
    return pl.pallas_call(
        kern,
        grid_spec=pltpu.PrefetchScalarGridSpec(
            num_scalar_prefetch=1,
            grid=(n // tc,),
            in_specs=[
                pl.BlockSpec(memory_space=pl.ANY),
                pl.BlockSpec((tc, d), lambda i, dest: (i, 0)),
                pl.BlockSpec((tc, TOP_K), lambda i, dest: (i, 0)),
                pl.BlockSpec((1, d), lambda i, dest: (0, 0)),
                pl.BlockSpec((1, d), lambda i, dest: (0, 0)),
            ],
            out_specs=[
                pl.BlockSpec((tc, d), lambda i, dest: (i, 0)),
                pl.BlockSpec((tc, d), lambda i, dest: (i, 0)),
            ],
            scratch_shapes=[pltpu.VMEM((tc, d // 2), jnp.uint32)] * 4 + [pltpu.SemaphoreType.DMA((2, 2))],
        ),
        out_shape=[jax.ShapeDtypeStruct((n, d), F32), jax.ShapeDtypeStruct((n, d), BF16)],
        compiler_params=_params("arbitrary"),
        name="combine_ln",
    )(dest, y, x, wts_tok, g, b)


def _rope_tables(seq):
    half = HEAD_DIM // 2
    inv = ROPE_THETA ** (-jnp.arange(half, dtype=F32) / half)
    ang = jnp.arange(seq, dtype=jnp.int32).astype(F32)[:, None] * inv[None, :]
    cos, sin = jnp.cos(ang), jnp.sin(ang)
    return jnp.concatenate([cos, cos], axis=-1), jnp.concatenate([-sin, sin], axis=-1)


def kernel(x, w_in, b_forget, w_out, ln1_g, ln1_b, w_router, router_bias, w_gate, w_up, w_down, ln2_g, ln2_b):
    batch, seq, d = x.shape
    depth = w_in.shape[0]
    fox_heads = b_forget.shape[-1]
    fox_w = fox_heads * HEAD_DIM
    moba_w = w_out.shape[1] - fox_w
    moba_heads = moba_w // HEAD_DIM
    n_exp = w_router.shape[1]
    n = batch * seq
    alpha = (2.0 * depth) ** 0.25
    qkv_w = 3 * (fox_w + moba_w)
    hb = HEAD_DIM

    cos, sin = _rope_tables(seq)
    w_router_t = w_router.T
    bias_col = router_bias.reshape(n_exp, 1)
    xf = x.reshape(n, d)
    xb = xf.astype(BF16)
    w_in_t = jnp.swapaxes(w_in, 1, 2)
    w_out_b = w_out.astype(BF16)

    for l in range(depth):
        w_f = jnp.pad(w_in_t[l, qkv_w:, :], ((0, LANES - fox_heads), (0, 0)))
        b_f = jnp.pad(b_forget[l], (0, LANES - fox_heads)).reshape(1, LANES)

        proj, kbar = _in_proj(xb, w_in_t, l, cos, sin, fox_w=fox_w, moba_w=moba_w, seq=seq)
        cp = _forget_cumsum(xb, w_f, b_f, batch=batch, seq=seq, heads=fox_heads)
        o_f = _fox_attention(proj, cp, batch=batch, seq=seq, heads=fox_heads,
                             k_col=fox_w // hb, v_col=2 * fox_w // hb)
        o_m = _moba_attention(proj, kbar.reshape(batch, seq // MOBA_BLOCK, moba_w),
                              batch=batch, seq=seq, heads=moba_heads,
                              q_col=3 * fox_w // hb, k_col=(3 * fox_w + moba_w) // hb,
                              v_col=(3 * fox_w + 2 * moba_w) // hb)
        x1, x1p = _out_proj_ln(o_f, o_m, w_out_b, l, xf,
                               ln1_g[l].reshape(1, d), ln1_b[l].reshape(1, d), alpha=alpha)

        idx, wts = _router(x1, w_router_t, bias_col)
        dest, row_tok, blk_expert, next_expert, n_used = _dispatch(idx, n_exp)
        h = _moe_up(x1p, w_gate, w_up, l, blk_expert, next_expert, row_tok, n_used)
        y = _moe_down(h, w_down, l, blk_expert, next_expert, n_used)
        xf, xb = _combine_ln(y, x1, wts.T, dest, ln2_g[l].reshape(1, d), ln2_b[l].reshape(1, d),
                             alpha=alpha)

    return xf.reshape(batch, seq, d)
```
